```python
import jax, jax.numpy as jnp
from jax import lax
import numpy as np

D_MODEL = 2048
BATCH = 8
SEQ = 4096
DEPTH = 1
DEC_BATCH = 32
DEC_SEQ = 64
PAST_LEN = 2048

CHUNK = 64
D_MIX = D_MODEL
C_A = D_MIX // 2
C_B = D_MIX - C_A
HEAD_DIM = 64
H_A = C_A // HEAD_DIM
H_B = C_B // HEAD_DIM
D_IN = 3 * C_A + 3 * C_B + H_B
R_W = max(32, int(round(1.8 * C_A ** 0.5 / 32)) * 32)
R_A = max(32, int(round(1.8 * C_A ** 0.5 / 32)) * 32)
R_G = max(32, int(round(0.6 * C_A ** 0.8 / 32)) * 32)
GN_EPS = 64e-5
RMS_EPS = 1e-6
Q_BLOCK = 128
N_GROUPS = 4
EXPERTS_PER_GROUP = 8
N_EXPERTS = N_GROUPS * EXPERTS_PER_GROUP
TOP_K = 2
D_EXPERT = D_MODEL // 4
MOE_BLOCK = 128

kernel_name = 'hymba_rwkv7_fox_hmoe_stream_layer'


def _rmsnorm(x, g):
    xf = x.astype(jnp.float32)
    y = xf * lax.rsqrt(jnp.mean(xf * xf, axis=-1, keepdims=True) + RMS_EPS)
    return y.astype(x.dtype) * g


def _rwkv7_group(h, rkv, shift_prev, wkv_prev, p):
    f32 = jnp.float32
    B, T, _ = h.shape
    x_last = shift_prev.astype(h.dtype)
    h_prev = jnp.concatenate([x_last[:, None], h[:, :-1]], axis=1)
    xx = h_prev - h
    mu = p['mu_wag']
    xw = h + xx * mu[0]
    xa = h + xx * mu[1]
    xg = h + xx * mu[2]
    w_rkv = p['w_in'][:, :3 * C_A]
    rkv_prev = jnp.concatenate([(x_last @ w_rkv)[:, None], rkv[:, :-1]], axis=1)
    rkv = rkv + p['mu_rkv'] * (rkv_prev - rkv)
    r, k, v = jnp.split(rkv, 3, axis=-1)
    w_log = -jax.nn.softplus(-(p['w0'] + jnp.tanh(xw @ p['w_lora_a']) @ p['w_lora_b']).astype(f32)) - 0.5
    decay = jnp.exp(-jnp.exp(w_log))
    a = jax.nn.sigmoid((p['a0'] + (xa @ p['a_lora_a']) @ p['a_lora_b']).astype(f32))
    g = jax.nn.sigmoid(xg @ p['g_lora_a']) @ p['g_lora_b']
    heads = lambda t: t.astype(f32).reshape(B, T, H_A, HEAD_DIM)
    kk = heads(k * p['k_k'])
    kk = kk / jnp.maximum(jnp.linalg.norm(kk, axis=-1, keepdims=True), 1e-12)
    kf = k.astype(f32) * (1.0 + (a - 1.0) * p['k_a'].astype(f32))
    r_h, k_h, v_h, w_h, a_h = heads(r), heads(kf), heads(v), heads(decay), heads(a)
    b_h = kk * a_h

    def step(S, inp):
        r_t, w_t, k_t, v_t, kk_t, b_t = inp
        sa = jnp.einsum('bhij,bhj->bhi', S, -kk_t)
        S = S * w_t[:, :, None, :] + sa[..., None] * b_t[:, :, None, :] + v_t[..., None] * k_t[:, :, None, :]
        return S, jnp.einsum('bhij,bhj->bhi', S, r_t)

    tm = lambda t: jnp.swapaxes(t, 0, 1)
    S_fin, y = lax.scan(step, wkv_prev.astype(f32), (tm(r_h), tm(w_h), tm(k_h), tm(v_h), tm(kk), tm(b_h)))
    y = tm(y)
    y_mu = jnp.mean(y, axis=-1, keepdims=True)
    y_var = jnp.mean(jnp.square(y - y_mu), axis=-1, keepdims=True)
    y = ((y - y_mu) * lax.rsqrt(y_var + GN_EPS)).reshape(B, T, C_A)
    y = y * p['ln_x_w'].astype(f32) + p['ln_x_b'].astype(f32)
    bonus = (jnp.sum(r_h * k_h * p['r_k'].astype(f32), axis=-1, keepdims=True) * v_h).reshape(B, T, C_A)
    out = ((y + bonus) * g.astype(f32)).astype(h.dtype)
    return out, h[:, -1].astype(shift_prev.dtype), S_fin.astype(wkv_prev.dtype)


def _fox_attention(q, k, v, F_q, F_k, q_offset):
    f32 = jnp.float32
    B, Tq, H, dh = q.shape
    Tk = k.shape[1]
    blk = min(Q_BLOCK, Tq)
    nb = Tq // blk
    qb = jnp.moveaxis(q.reshape(B, nb, blk, H, dh), 1, 0)
    Fqb = jnp.moveaxis(F_q.reshape(B, nb, blk, H), 1, 0)
    Fk_t = jnp.swapaxes(F_k, 1, 2)
    k_pos = jnp.arange(Tk)
    scale = dh ** -0.5

    def one(args):
        q_i, F_i, i = args
        s = jnp.einsum('bqhd,bkhd->bhqk', q_i, k).astype(f32) * scale
        s = s + jnp.swapaxes(F_i, 1, 2)[..., None] - Fk_t[:, :, None, :]
        q_pos = q_offset + i * blk + jnp.arange(blk)
        s = jnp.where(k_pos[None, :] <= q_pos[:, None], s, -jnp.inf)
        pr = jax.nn.softmax(s, axis=-1)
        return jnp.einsum('bhqk,bkhd->bqhd', pr.astype(v.dtype), v)

    o = lax.map(one, (qb, Fqb, jnp.arange(nb)))
    return jnp.moveaxis(o, 0, 1).reshape(B, Tq, H, dh)


def _fox_group(qkvf, k_past, v_past, logf_past, b_f):
    f32 = jnp.float32
    B, T, _ = qkvf.shape
    q = qkvf[..., :C_B].reshape(B, T, H_B, HEAD_DIM)
    k = qkvf[..., C_B:2 * C_B].reshape(B, T, H_B, HEAD_DIM)
    v = qkvf[..., 2 * C_B:3 * C_B].reshape(B, T, H_B, HEAD_DIM)
    logf = jax.nn.log_sigmoid(qkvf[..., 3 * C_B:].astype(f32) + b_f.astype(f32))
    n_past = k_past.shape[1]
    k_all = jnp.concatenate([k_past.astype(k.dtype), k], axis=1)
    v_all = jnp.concatenate([v_past.astype(v.dtype), v], axis=1)
    F = jnp.cumsum(jnp.concatenate([logf_past.astype(f32), logf], axis=1), axis=1)
    o = _fox_attention(q, k_all, v_all, F[:, n_past:], F, n_past)
    return (o.reshape(B, T, C_B), k.astype(k_past.dtype), v.astype(v_past.dtype),
            logf.astype(logf_past.dtype))


def _hier_moe(x, w_rg, b_rg, w_re, b_re, w1, w3, w2):
    f32 = jnp.float32
    T, D = x.shape
    lg = (x @ w_rg + b_rg).astype(f32)
    pg_top = jnp.max(jax.nn.softmax(lg, axis=-1), axis=-1)
    gi = jnp.argmax(lg, axis=-1)
    le = (x @ w_re + b_re).astype(f32).reshape(T, N_GROUPS, EXPERTS_PER_GROUP)
    le = le[jnp.arange(T), gi]
    pe_top, ei = lax.top_k(jax.nn.softmax(le, axis=-1), TOP_K)
    pe_top = pe_top / jnp.sum(pe_top, axis=-1, keepdims=True)
    n = T * TOP_K
    wt = (pg_top[:, None] * pe_top).reshape(n)
    eid = (gi[:, None] * EXPERTS_PER_GROUP + ei).reshape(n).astype(jnp.int32)
    tok = jnp.repeat(jnp.arange(T, dtype=jnp.int32), TOP_K)
    order = jnp.argsort(eid)
    eid_s, tok_s, wt_s = eid[order], tok[order], wt[order]
    counts = jnp.zeros((N_EXPERTS,), jnp.int32).at[eid].add(1)
    start = jnp.cumsum(counts) - counts
    padded = (counts + MOE_BLOCK - 1) // MOE_BLOCK * MOE_BLOCK
    pad_end = jnp.cumsum(padded)
    pad_start = pad_end - padded
    dest = pad_start[eid_s] + jnp.arange(n, dtype=jnp.int32) - start[eid_s]
    n_blocks = (n + N_EXPERTS * (MOE_BLOCK - 1) + MOE_BLOCK - 1) // MOE_BLOCK
    rows = n_blocks * MOE_BLOCK
    row_tok = jnp.full((rows,), T, jnp.int32).at[dest].set(tok_s)
    row_wt = jnp.zeros((rows,), f32).at[dest].set(wt_s)
    blk_eid = jnp.minimum(jnp.searchsorted(pad_end, jnp.arange(n_blocks, dtype=jnp.int32) * MOE_BLOCK,
                                           side='right'), N_EXPERTS - 1)
    x_pad = jnp.concatenate([x, jnp.zeros((1, D), x.dtype)], axis=0)
    xin = x_pad[row_tok].reshape(n_blocks, MOE_BLOCK, D)

    def expert_block(args):
        xb, e = args
        hb = jax.nn.silu(xb @ w1[e]) * (xb @ w3[e])
        return hb @ w2[e]

    yb = lax.map(expert_block, (xin, blk_eid)).reshape(rows, D)
    out = jnp.zeros((T + 1, D), f32).at[row_tok].add(yb.astype(f32) * row_wt[:, None])
    return out[:T].astype(x.dtype)


def _layer(x, c, shift_prev, wkv_prev, k_past, v_past, logf_past, p):
    B, T, D = x.shape
    mod = (jax.nn.silu(c) @ p['w_ada'] + p['b_ada'])[:, None, :]
    sh_a, sc_a, gt_a, sh_f, sc_f, gt_f = jnp.split(mod, 6, axis=-1)
    h = _rmsnorm(x, p['g_pre_mix']) * (1.0 + sc_a) + sh_a
    proj = h @ p['w_in']
    y_a, shift_new, wkv_new = _rwkv7_group(h, proj[..., :3 * C_A], shift_prev, wkv_prev, p)
    y_b, k_new, v_new, logf_new = _fox_group(proj[..., 3 * C_A:], k_past, v_past, logf_past, p['b_f'])
    o = jnp.concatenate([y_a, y_b], axis=-1) @ p['w_out']
    x = x + gt_a * _rmsnorm(o, p['g_post_mix'])
    h2 = _rmsnorm(x, p['g_pre_ffn']) * (1.0 + sc_f) + sh_f
    f = _hier_moe(h2.reshape(B * T, D), p['w_rg'], p['b_rg'], p['w_re'], p['b_re'],
                  p['w1'], p['w3'], p['w2']).reshape(B, T, D)
    x = x + gt_f * _rmsnorm(f, p['g_post_ffn'])
    return x, shift_new, wkv_new, k_new, v_new, logf_new


def setup_inputs(seed: int = 0) -> dict:
    key = jax.random.key(seed)
    ks = iter(jax.random.split(key, 64))
    f32 = jnp.float32
    L = DEPTH

    def nrm(shape, scale=1.0):
        return jax.random.normal(next(ks), shape, f32) * scale

    def unif(shape, lo, hi):
        return jax.random.uniform(next(ks), shape, f32, lo, hi)

    sd = D_MODEL ** -0.5
    return {
        'x_prompt': nrm((BATCH, SEQ, D_MODEL)),
        'x_sample': nrm((DEC_BATCH, DEC_SEQ, D_MODEL)),
        'state_shift': nrm((L, DEC_BATCH, D_MODEL)),
        'state_wkv': nrm((L, DEC_BATCH, H_A, HEAD_DIM, HEAD_DIM), 0.1),
        'cache_k': nrm((L, DEC_BATCH, PAST_LEN, H_B, HEAD_DIM)),
        'cache_v': nrm((L, DEC_BATCH, PAST_LEN, H_B, HEAD_DIM)),
        'cache_logf': jax.nn.log_sigmoid(nrm((L, DEC_BATCH, PAST_LEN, H_B)) + 2.5),
        'c_prompt': nrm((BATCH, D_MODEL)),
        'c_sample': nrm((DEC_BATCH, D_MODEL)),
        'w_ada': nrm((L, D_MODEL, 6 * D_MODEL), 0.5 * sd),
        'b_ada': nrm((L, 6 * D_MODEL), 0.02),
        'g_pre_mix': 1.0 + nrm((L, D_MODEL), 0.05),
        'g_post_mix': 1.0 + nrm((L, D_MODEL), 0.05),
        'g_pre_ffn': 1.0 + nrm((L, D_MODEL), 0.05),
        'g_post_ffn': 1.0 + nrm((L, D_MODEL), 0.05),
        'w_in': nrm((L, D_MODEL, D_IN), sd),
        'b_f': unif((L, H_B), 1.0, 4.0),
        'mu_rkv': unif((L, 3 * C_A), 0.0, 1.0),
        'mu_wag': unif((L, 3, D_MODEL), 0.0, 1.0),
        'w0': unif((L, C_A), -6.0, -1.0),
        'w_lora_a': nrm((L, D_MODEL, R_W), sd),
        'w_lora_b': nrm((L, R_W, C_A), 0.1 * R_W ** -0.5),
        'a0': nrm((L, C_A), 0.5),
        'a_lora_a': nrm((L, D_MODEL, R_A), sd),
        'a_lora_b': nrm((L, R_A, C_A), 0.1 * R_A ** -0.5),
        'g_lora_a': nrm((L, D_MODEL, R_G), sd),
        'g_lora_b': nrm((L, R_G, C_A), R_G ** -0.5),
        'k_k': 0.85 + nrm((L, C_A), 0.05),
        'k_a': 1.0 + nrm((L, C_A), 0.05),
        'r_k': nrm((L, H_A, HEAD_DIM), 0.1),
        'ln_x_w': 1.0 + nrm((L, C_A), 0.05),
        'ln_x_b': nrm((L, C_A), 0.02),
        'w_out': nrm((L, D_MIX, D_MODEL), D_MIX ** -0.5),
        'w_rg': nrm((L, D_MODEL, N_GROUPS), sd),
        'b_rg': nrm((L, N_GROUPS), 0.01),
        'w_re': nrm((L, D_MODEL, N_EXPERTS), sd),
        'b_re': nrm((L, N_EXPERTS), 0.01),
        'w1': nrm((L, N_EXPERTS, D_MODEL, D_EXPERT), sd),
        'w3': nrm((L, N_EXPERTS, D_MODEL, D_EXPERT), sd),
        'w2': nrm((L, N_EXPERTS, D_EXPERT, D_MODEL), D_EXPERT ** -0.5),
    }


def reference(x_prompt, x_sample, state_shift, state_wkv, cache_k, cache_v, cache_logf,
              c_prompt, c_sample, w_ada, b_ada, g_pre_mix, g_post_mix, g_pre_ffn, g_post_ffn,
              w_in, b_f, mu_rkv, mu_wag, w0, w_lora_a, w_lora_b, a0, a_lora_a, a_lora_b,
              g_lora_a, g_lora_b, k_k, k_a, r_k, ln_x_w, ln_x_b, w_out,
              w_rg, b_rg, w_re, b_re, w1, w3, w2):
    xp, xs = x_prompt, x_sample
    bp = x_prompt.shape[0]
    p_shift, p_wkv, p_k, p_v, p_lf = [], [], [], [], []
    s_shift, s_wkv, s_k, s_v, s_lf = [], [], [], [], []
    for l in range(DEPTH):
        p = dict(w_ada=w_ada[l], b_ada=b_ada[l], g_pre_mix=g_pre_mix[l], g_post_mix=g_post_mix[l],
                 g_pre_ffn=g_pre_ffn[l], g_post_ffn=g_post_ffn[l], w_in=w_in[l], b_f=b_f[l],
                 mu_rkv=mu_rkv[l], mu_wag=mu_wag[l], w0=w0[l], w_lora_a=w_lora_a[l], w_lora_b=w_lora_b[l],
                 a0=a0[l], a_lora_a=a_lora_a[l], a_lora_b=a_lora_b[l], g_lora_a=g_lora_a[l],
                 g_lora_b=g_lora_b[l], k_k=k_k[l], k_a=k_a[l], r_k=r_k[l], ln_x_w=ln_x_w[l],
                 ln_x_b=ln_x_b[l], w_out=w_out[l], w_rg=w_rg[l], b_rg=b_rg[l], w_re=w_re[l],
                 b_re=b_re[l], w1=w1[l], w3=w3[l], w2=w2[l])
        xp, a1, a2, a3, a4, a5 = _layer(
            xp, c_prompt,
            jnp.zeros((bp, D_MODEL), state_shift.dtype),
            jnp.zeros((bp, H_A, HEAD_DIM, HEAD_DIM), state_wkv.dtype),
            jnp.zeros((bp, 0, H_B, HEAD_DIM), cache_k.dtype),
            jnp.zeros((bp, 0, H_B, HEAD_DIM), cache_v.dtype),
            jnp.zeros((bp, 0, H_B), cache_logf.dtype), p)
        p_shift.append(a1); p_wkv.append(a2); p_k.append(a3); p_v.append(a4); p_lf.append(a5)
        xs, b1, b2, b3, b4, b5 = _layer(
            xs, c_sample, state_shift[l], state_wkv[l], cache_k[l], cache_v[l], cache_logf[l], p)
        s_shift.append(b1); s_wkv.append(b2); s_k.append(b3); s_v.append(b4); s_lf.append(b5)
    return (xp, xs,
            jnp.stack(p_shift), jnp.stack(p_wkv), jnp.stack(p_k), jnp.stack(p_v), jnp.stack(p_lf),
            jnp.stack(s_shift), jnp.stack(s_wkv), jnp.stack(s_k), jnp.stack(s_v), jnp.stack(s_lf))
```

```python
import functools
import math

import jax
import jax.numpy as jnp
from jax import lax
from jax.experimental import pallas as pl
from jax.experimental.pallas import tpu as pltpu

F32 = jnp.float32
BF16 = jnp.bfloat16
I32 = jnp.int32

HEAD_DIM = 64
LANES = 128
PAIR = 2 * HEAD_DIM
RMS_EPS = 1e-6
GN_EPS = 64e-5
N_GROUPS = 4
EXPERTS_PER_GROUP = 8
N_EXPERTS = N_GROUPS * EXPERTS_PER_GROUP
TOP_K = 2
CHUNK = 64
VMEM_LIMIT_BYTES = 56 * 1024 * 1024
ROW_TILE = 512
MOE_BLOCK = 256


def _cparams(*sem):
    return pltpu.CompilerParams(dimension_semantics=sem, vmem_limit_bytes=VMEM_LIMIT_BYTES)


def _dot(a, b):
    return jnp.dot(a, b, preferred_element_type=F32)


def _dot_nt(a, b):
    return lax.dot_general(a, b, (((1,), (1,)), ((), ())), preferred_element_type=F32)


def _dot_tn(a, b):
    return lax.dot_general(a, b, (((0,), (0,)), ((), ())), preferred_element_type=F32)


def _split3(x):
    hi = x.astype(BF16)
    r1 = x - hi.astype(F32)
    mid = r1.astype(BF16)
    lo = (r1 - mid.astype(F32)).astype(BF16)
    return hi, mid, lo


def _dot3_exact_rhs(x, m):
    hi, mid, lo = _split3(x)
    return _dot(hi, m) + _dot(mid, m) + _dot(lo, m)


def _sigmoid(x):
    return 1.0 / (1.0 + jnp.exp(-x))


def _log_sigmoid(x):
    return jnp.minimum(x, 0.0) - jnp.log(1.0 + jnp.exp(-jnp.abs(x)))


def _mod_norm(x, g, scale, shift):
    y = x * lax.rsqrt(jnp.mean(x * x, axis=-1, keepdims=True) + RMS_EPS)
    return (y * g) * (1.0 + scale) + shift


def _row_tiling(batch, seq):
    if seq >= ROW_TILE:
        assert seq % ROW_TILE == 0
        return 1, ROW_TILE
    assert ROW_TILE % seq == 0 and batch % (ROW_TILE // seq) == 0
    return ROW_TILE // seq, seq


def _ada_kernel(c_ref, w_ref, b_ref, o_ref):
    c = c_ref[...]
    s = c * _sigmoid(c)
    o_ref[...] = _dot(s.astype(BF16), w_ref[...].astype(BF16)) + b_ref[...]


def _ada_mod(c_all, w_ada, b_ada):
    m, d = c_all.shape
    n = w_ada.shape[1]
    tn = 1024
    return pl.pallas_call(
        _ada_kernel,
        out_shape=jax.ShapeDtypeStruct((m, n), F32),
        grid=(n // tn,),
        in_specs=[pl.BlockSpec((m, d), lambda j: (0, 0)),
                  pl.BlockSpec((d, tn), lambda j: (0, j)),
                  pl.BlockSpec((1, tn), lambda j: (0, j))],
        out_specs=pl.BlockSpec((m, tn), lambda j: (0, j)),
        compiler_params=_cparams("arbitrary"),
        name="ada_mod",
    )(c_all, w_ada, b_ada.reshape(1, n))


def _proj_kernel(x_ref, sc_ref, sh_ref, g_ref, w_ref, *refs):
    outs, hs_ref = refs[:-1], refs[-1]
    j = pl.program_id(2)
    bb, tt, d = x_ref.shape

    @pl.when(j == 0)
    def _():
        h = _mod_norm(x_ref[...], g_ref[...], sc_ref[...], sh_ref[...])
        hs_ref[...] = h.reshape(bb * tt, d).astype(BF16)

    acc = _dot(hs_ref[...], w_ref[...])
    for idx, o_ref in enumerate(outs):
        @pl.when(j == idx)
        def _(o_ref=o_ref):
            o_ref[...] = acc.reshape(o_ref.shape).astype(o_ref.dtype)


def _in_proj(x, sc, sh, g, w_main, out_dtypes):
    b, t, d = x.shape
    n_out = len(out_dtypes)
    cw = w_main.shape[1] // n_out
    bb, tt = _row_tiling(b, t)
    row_spec = pl.BlockSpec((bb, tt, d), lambda ib, it, j: (ib, it, 0))
    mod_spec = pl.BlockSpec((bb, 1, d), lambda ib, it, j: (ib, 0, 0))
    out_spec = pl.BlockSpec((bb, tt, cw), lambda ib, it, j: (ib, it, 0))
    return pl.pallas_call(
        _proj_kernel,
        out_shape=[jax.ShapeDtypeStruct((b, t, cw), dt) for dt in out_dtypes],
        grid=(b // bb, t // tt, n_out),
        in_specs=[row_spec, mod_spec, mod_spec,
                  pl.BlockSpec((1, 1, d), lambda ib, it, j: (0, 0, 0)),
                  pl.BlockSpec((d, cw), lambda ib, it, j: (0, j))],
        out_specs=[out_spec] * n_out,
        scratch_shapes=[pltpu.VMEM((bb * tt, d), BF16)],
        compiler_params=_cparams("arbitrary", "arbitrary", "arbitrary"),
        name="in_proj",
    )(x, sc, sh, g.reshape(1, 1, d), w_main)


def _lora_kernel(x_ref, sc_ref, sh_ref, g_ref, prev_ref, muw_ref, mua_ref, mug_ref,
                 aw_ref, bw_ref, w0_ref, aa_ref, ba_ref, a0_ref, ag_ref, bg_ref, wf_ref, bf_ref,
                 wl_ref, a_ref, gate_ref, lf_ref, shn_ref, carry_ref):
    it = pl.program_id(1)
    bb, tt, d = x_ref.shape
    rows = bb * tt
    h = _mod_norm(x_ref[...], g_ref[...], sc_ref[...], sh_ref[...])

    @pl.when(it == 0)
    def _():
        carry_ref[...] = prev_ref[...]

    rolled = pltpu.roll(h.reshape(rows, d), 1, 0).reshape(bb, tt, d)
    tpos = lax.broadcasted_iota(I32, (bb, tt, d), 1)
    h_prev = jnp.where(tpos == 0, carry_ref[...], rolled)
    last = h[:, tt - 1:tt, :]
    carry_ref[...] = last
    shn_ref[...] = last

    xx = h_prev - h

    def mix(mu_ref):
        return (h + xx * mu_ref[...]).reshape(rows, d).astype(BF16)

    ca = wl_ref.shape[-1]
    lw = jnp.tanh(_dot(mix(muw_ref), aw_ref[...]))
    w = w0_ref[...] + _dot(lw.astype(BF16), bw_ref[...])
    wl_ref[...] = (-math.exp(-0.5) * _sigmoid(w)).reshape(bb, tt, ca)
    la = _dot(mix(mua_ref), aa_ref[...])
    a_ref[...] = _sigmoid(a0_ref[...] + _dot(la.astype(BF16), ba_ref[...])).reshape(bb, tt, ca)
    lg = _sigmoid(_dot(mix(mug_ref), ag_ref[...]))
    gate_ref[...] = _dot(lg.astype(BF16), bg_ref[...]).reshape(bb, tt, ca).astype(gate_ref.dtype)
    nh = lf_ref.shape[-1]
    z = _dot(h.reshape(rows, d).astype(BF16), wf_ref[...]) + bf_ref[...]
    lf_ref[...] = _log_sigmoid(z)[:, :nh].reshape(bb, tt, nh)


def _pad_to(x, axis, size):
    pad = [(0, 0)] * x.ndim
    pad[axis] = (0, size - x.shape[axis])
    return jnp.pad(x, pad)


def _round_up(n, m):
    return (n + m - 1) // m * m


def _lora_heads(x, sc, sh, g, shift_prev, p):
    b, t, d = x.shape
    ca = p["w0"].shape[-1]
    nh = p["b_f"].shape[-1]
    bb, tt = _row_tiling(b, t)

    def lora_pair(a_w, b_w):
        r = _round_up(a_w.shape[1], LANES)
        return _pad_to(a_w, 1, r).astype(BF16), _pad_to(b_w, 0, r).astype(BF16)

    aw, bw = lora_pair(p["w_lora_a"], p["w_lora_b"])
    aa, ba = lora_pair(p["a_lora_a"], p["a_lora_b"])
    ag, bg = lora_pair(p["g_lora_a"], p["g_lora_b"])
    wf = _pad_to(p["w_forget"], 1, LANES).astype(BF16)
    bf = _pad_to(p["b_f"].reshape(1, nh), 1, LANES)
    mu = p["mu_wag"].reshape(3, 1, 1, d)

    row_spec = pl.BlockSpec((bb, tt, d), lambda ib, it: (ib, it, 0))
    mod_spec = pl.BlockSpec((bb, 1, d), lambda ib, it: (ib, 0, 0))
    vec_spec = pl.BlockSpec((1, 1, d), lambda ib, it: (0, 0, 0))

    def full(a):
        return pl.BlockSpec(a.shape, lambda ib, it: (0,) * a.ndim)

    ca_spec = pl.BlockSpec((bb, tt, ca), lambda ib, it: (ib, it, 0))
    weights = [aw, bw, p["w0"].reshape(1, ca), aa, ba, p["a0"].reshape(1, ca), ag, bg, wf, bf]
    return pl.pallas_call(
        _lora_kernel,
        out_shape=[jax.ShapeDtypeStruct((b, t, ca), F32), jax.ShapeDtypeStruct((b, t, ca), F32),
                   jax.ShapeDtypeStruct((b, t, ca), BF16), jax.ShapeDtypeStruct((b, t, nh), F32),
                   jax.ShapeDtypeStruct((b, 1, d), F32)],
        grid=(b // bb, t // tt),
        in_specs=[row_spec, mod_spec, mod_spec, vec_spec, mod_spec, vec_spec, vec_spec, vec_spec]
                 + [full(a) for a in weights],
        out_specs=[ca_spec, ca_spec, ca_spec,
                   pl.BlockSpec((bb, tt, nh), lambda ib, it: (ib, it, 0)),
                   pl.BlockSpec((bb, 1, d), lambda ib, it: (ib, 0, 0))],
        scratch_shapes=[pltpu.VMEM((bb, 1, d), F32)],
        compiler_params=_cparams("arbitrary", "arbitrary"),
        name="lora_heads",
    )(x, sc, sh, g.reshape(1, 1, d), shift_prev, mu[0], mu[1], mu[2], *weights)


def _cumsum_kernel(x_ref, o_ref, carry_ref):
    it = pl.program_id(1)
    bb, nh, tl = x_ref.shape

    @pl.when(it == 0)
    def _():
        carry_ref[...] = jnp.zeros_like(carry_ref)

    r = lax.broadcasted_iota(I32, (tl, tl), 0)
    c = lax.broadcasted_iota(I32, (tl, tl), 1)
    upper = jnp.where(r <= c, 1.0, 0.0).astype(BF16)
    cs = _dot3_exact_rhs(x_ref[...].reshape(bb * nh, tl), upper) + carry_ref[...]
    o_ref[...] = cs.reshape(bb, nh, tl)
    carry_ref[...] = cs[:, tl - 1:tl]


def _cumsum_lanes(x):
    b, nh, l = x.shape
    tl = l if l <= 2304 else 1024
    assert l % tl == 0 and nh % 8 == 0
    bb = math.gcd(b, 8)
    return pl.pallas_call(
        _cumsum_kernel,
        out_shape=jax.ShapeDtypeStruct((b, nh, l), F32),
        grid=(b // bb, l // tl),
        in_specs=[pl.BlockSpec((bb, nh, tl), lambda ib, it: (ib, 0, it))],
        out_specs=pl.BlockSpec((bb, nh, tl), lambda ib, it: (ib, 0, it)),
        scratch_shapes=[pltpu.VMEM((bb * nh, 1), F32)],
        compiler_params=_cparams("arbitrary", "arbitrary"),
        name="logf_cumsum",
    )(x)


NEG_BIG = -1e30


def _head_masked(q):
    lane = lax.broadcasted_iota(I32, q.shape, 1)
    first = lane < HEAD_DIM
    zero = jnp.zeros_like(q)
    return first, (jnp.where(first, q, zero), jnp.where(first, zero, q))


def _fox_prompt_kernel(q_ref, k_ref, v_ref, fq_ref, fk_ref, o_ref, kbf_ref, vbf_ref):
    qi = pl.program_id(2)
    tq = q_ref.shape[0]
    tk = tq

    @pl.when(qi == 0)
    def _():
        kbf_ref[...] = k_ref[...].astype(BF16)
        vbf_ref[...] = v_ref[...].astype(BF16)

    first, qs = _head_masked(q_ref[...])
    fq = fq_ref[...]
    scale = HEAD_DIM ** -0.5
    row = lax.broadcasted_iota(I32, (tq, tk), 0)
    col = lax.broadcasted_iota(I32, (tq, tk), 1)

    def block(kj, carry, masked):
        m_prev, l_prev, acc = carry
        start = pl.multiple_of(kj * tk, tk)
        ks = kbf_ref[pl.ds(start, tk), :]
        vs = vbf_ref[pl.ds(start, tk), :]
        fk = fk_ref[:, pl.ds(start, tk)]
        m_new, l_new, alpha, pv = [], [], [], []
        for e in range(2):
            s = _dot_nt(qs[e], ks) * scale + (fq[:, e:e + 1] - fk[e:e + 1, :])
            if masked:
                s = jnp.where(col <= row, s, NEG_BIG)
            m_e = jnp.maximum(m_prev[e], jnp.max(s, axis=1, keepdims=True))
            p = jnp.exp(s - m_e)
            a_e = jnp.exp(m_prev[e] - m_e)
            m_new.append(m_e)
            alpha.append(a_e)
            l_new.append(a_e * l_prev[e] + jnp.sum(p, axis=1, keepdims=True))
            pv.append(_dot(p.astype(BF16), vs))
        acc = acc * jnp.where(first, alpha[0], alpha[1]) + jnp.where(first, pv[0], pv[1])
        return tuple(m_new), tuple(l_new), acc

    neg = jnp.full((tq, 1), NEG_BIG, F32)
    zero = jnp.zeros((tq, 1), F32)
    init = ((neg, neg), (zero, zero), jnp.zeros((tq, PAIR), F32))
    carry = lax.fori_loop(0, qi, lambda kj, c: block(kj, c, False), init)
    _, l_fin, acc = block(qi, carry, True)
    o_ref[...] = (acc / jnp.where(first, l_fin[0], l_fin[1])).astype(o_ref.dtype)


def _fox_prompt(q, k, v, fq, fk):
    b, t, cb = q.shape
    hp = cb // PAIR
    tq = min(t, 512)
    return pl.pallas_call(
        _fox_prompt_kernel,
        out_shape=jax.ShapeDtypeStruct((b, t, cb), BF16),
        grid=(b, hp, t // tq),
        in_specs=[pl.BlockSpec((None, tq, PAIR), lambda ib, ih, iq: (ib, iq, ih)),
                  pl.BlockSpec((None, t, PAIR), lambda ib, ih, iq: (ib, 0, ih)),
                  pl.BlockSpec((None, t, PAIR), lambda ib, ih, iq: (ib, 0, ih)),
                  pl.BlockSpec((None, None, tq, 2), lambda ib, ih, iq: (ib, ih, iq, 0)),
                  pl.BlockSpec((None, None, 2, t), lambda ib, ih, iq: (ib, ih, 0, 0))],
        out_specs=pl.BlockSpec((None, tq, PAIR), lambda ib, ih, iq: (ib, iq, ih)),
        scratch_shapes=[pltpu.VMEM((t, PAIR), BF16), pltpu.VMEM((t, PAIR), BF16)],
        compiler_params=_cparams("arbitrary", "arbitrary", "arbitrary"),
        name="fox_prompt",
    )(q, k, v, fq, fk)


def _fox_sample_kernel(q_ref, kn_ref, vn_ref, kc_ref, vc_ref, fq_ref, fk_ref, o_ref):
    ts = q_ref.shape[0]
    past = kc_ref.shape[0]
    n_pairs = q_ref.shape[1] // PAIR
    scale = HEAD_DIM ** -0.5
    row = lax.broadcasted_iota(I32, (ts, ts), 0)
    col = lax.broadcasted_iota(I32, (ts, ts), 1)
    for pp in range(n_pairs):
        lanes = slice(pp * PAIR, (pp + 1) * PAIR)
        first, qs = _head_masked(q_ref[:, lanes])
        kc = kc_ref[:, lanes].astype(BF16)
        vc = vc_ref[:, lanes].astype(BF16)
        kn = kn_ref[:, lanes].astype(BF16)
        vn = vn_ref[:, lanes].astype(BF16)
        outs = []
        for e in range(2):
            fq = fq_ref[pp, :, e:e + 1]
            s_p = _dot_nt(qs[e], kc) * scale + (fq - fk_ref[pp, e:e + 1, 0:past])
            s_n = _dot_nt(qs[e], kn) * scale + (fq - fk_ref[pp, e:e + 1, past:past + ts])
            s_n = jnp.where(col <= row, s_n, NEG_BIG)
            m = jnp.maximum(jnp.max(s_p, axis=1, keepdims=True), jnp.max(s_n, axis=1, keepdims=True))
            p_p = jnp.exp(s_p - m)
            p_n = jnp.exp(s_n - m)
            l = jnp.sum(p_p, axis=1, keepdims=True) + jnp.sum(p_n, axis=1, keepdims=True)
            outs.append((_dot(p_p.astype(BF16), vc) + _dot(p_n.astype(BF16), vn)) / l)
        o_ref[:, lanes] = jnp.where(first, outs[0], outs[1]).astype(o_ref.dtype)


def _fox_sample(q, kn, vn, kc, vc, fq, fk):
    b, ts, cb = q.shape
    past = kc.shape[1]
    lp = fk.shape[-1]
    lb = min(cb, 4 * PAIR)
    npb = lb // PAIR
    new_spec = pl.BlockSpec((None, ts, lb), lambda ib, ig: (ib, 0, ig))
    past_spec = pl.BlockSpec((None, past, lb), lambda ib, ig: (ib, 0, ig))
    return pl.pallas_call(
        _fox_sample_kernel,
        out_shape=jax.ShapeDtypeStruct((b, ts, cb), BF16),
        grid=(b, cb // lb),
        in_specs=[new_spec, new_spec, new_spec, past_spec, past_spec,
                  pl.BlockSpec((None, npb, ts, 2), lambda ib, ig: (ib, ig, 0, 0)),
                  pl.BlockSpec((None, npb, 2, lp), lambda ib, ig: (ib, ig, 0, 0))],
        out_specs=new_spec,
        compiler_params=_cparams("arbitrary", "arbitrary"),
        name="fox_sample",
    )(q, kn, vn, kc, vc, fq, fk)


def _forget_layouts(logf_new, logf_past):
    b, t, nh = logf_new.shape
    lt = jnp.swapaxes(logf_new, 1, 2)
    past = 0
    if logf_past is not None:
        past = logf_past.shape[1]
        lt = jnp.concatenate([jnp.swapaxes(logf_past, 1, 2), lt], axis=2)
    lp = _round_up(past + t, LANES)
    ft = _cumsum_lanes(_pad_to(lt, 2, lp))
    fk = ft.reshape(b, nh // 2, 2, lp)
    fq = jnp.swapaxes(fk[..., past:past + t], 2, 3)
    return fq, fk


def _mm(a, b):
    return _dot(a.astype(BF16), b.astype(BF16))


def _mm_nt(a, b):
    return _dot_nt(a.astype(BF16), b.astype(BF16))


def _mm_tn(a, b):
    return _dot_tn(a.astype(BF16), b.astype(BF16))


def _rwkv_kernel(r_ref, k_ref, v_ref, wl_ref, a_ref, g_ref, rl_ref, kl_ref, vl_ref, s0_ref,
                 mur_ref, muk_ref, muv_ref, kk_ref, ka_ref, rk_ref, lnw_ref, lnb_ref,
                 y_ref, sout_ref, st_ref, prev_ref):
    itb = pl.program_id(2)
    tb, lanes = r_ref.shape
    n_pairs = lanes // PAIR
    c = min(CHUNK, tb)

    @pl.when(itb == 0)
    def _():
        st_ref[...] = s0_ref[...]
        prev_ref[0:1, :] = rl_ref[...]
        prev_ref[1:2, :] = kl_ref[...]
        prev_ref[2:3, :] = vl_ref[...]

    ri = lax.broadcasted_iota(I32, (c, c), 0)
    ci = lax.broadcasted_iota(I32, (c, c), 1)
    strict = ri > ci
    incl = ri >= ci
    tri_ones = jnp.where(incl, 1.0, 0.0).astype(BF16)
    li = lax.broadcasted_iota(I32, (PAIR, PAIR), 0) // HEAD_DIM
    lj = lax.broadcasted_iota(I32, (PAIR, PAIR), 1) // HEAD_DIM
    same_head = li == lj
    head_ones = jnp.where(same_head, 1.0, 0.0).astype(BF16)
    first = lax.broadcasted_iota(I32, (c, PAIR), 1) < HEAD_DIM
    row0 = lax.broadcasted_iota(I32, (c, PAIR), 0) == 0
    ones_cp = jnp.ones((c, PAIR), BF16)

    def head_sum(x):
        return _dot3_exact_rhs(x, head_ones)

    def shifted(x, prev_row):
        return jnp.where(row0, prev_row, pltpu.roll(x, 1, 0))

    def chunk(ic, carry):
        t0 = pl.multiple_of(ic * c, c)
        rows = pl.ds(t0, c)
        for pp in range(n_pairs):
            ln = slice(pp * PAIR, (pp + 1) * PAIR)
            r_raw, k_raw, v_raw = r_ref[rows, ln], k_ref[rows, ln], v_ref[rows, ln]
            r = r_raw + mur_ref[:, ln] * (shifted(r_raw, prev_ref[0:1, ln]) - r_raw)
            k = k_raw + muk_ref[:, ln] * (shifted(k_raw, prev_ref[1:2, ln]) - k_raw)
            v = v_raw + muv_ref[:, ln] * (shifted(v_raw, prev_ref[2:3, ln]) - v_raw)
            prev_ref[0:1, ln] = r_raw[c - 1:c, :]
            prev_ref[1:2, ln] = k_raw[c - 1:c, :]
            prev_ref[2:3, ln] = v_raw[c - 1:c, :]
            wl = wl_ref[rows, ln]
            a = a_ref[rows, ln]

            kk = k * kk_ref[:, ln]
            kk = kk / jnp.maximum(jnp.sqrt(head_sum(kk * kk)), 1e-12)
            kf = k * (1.0 + (a - 1.0) * ka_ref[:, ln])
            b = kk * a
            bonus = head_sum(r * kf * rk_ref[:, ln]) * v

            w_hi, w_mid, w_lo = _split3(wl)
            lcum = _dot(tri_ones, w_hi) + _dot(tri_ones, w_mid) + _dot(tri_ones, w_lo)
            ltot = lcum[c - 1:c, :]
            suffix = jnp.exp(ltot - lcum)
            inv = jnp.exp(-lcum)
            r_bar = r * jnp.exp(lcum)
            k_bar = kf * inv
            b_bar = b * inv
            a_bar = -kk * jnp.exp(lcum - wl)
            decay_mat = jnp.exp(_dot_tn(w_hi, ones_cp) + _dot_tn(w_mid, ones_cp) + _dot_tn(w_lo, ones_cp))

            st = st_ref[pp]
            zero = jnp.zeros_like(a_bar)
            a_h = (jnp.where(first, a_bar, zero), jnp.where(first, zero, a_bar))
            r_h = (jnp.where(first, r_bar, zero), jnp.where(first, zero, r_bar))
            a_ab = [jnp.where(strict, _mm_nt(a_h[e], b_bar), 0.0) for e in range(2)]
            a_ak = [jnp.where(strict, _mm_nt(a_h[e], k_bar), 0.0) for e in range(2)]
            r_b = [jnp.where(incl, _mm_nt(r_h[e], b_bar), 0.0) for e in range(2)]
            r_k = [jnp.where(incl, _mm_nt(r_h[e], k_bar), 0.0) for e in range(2)]

            rhs = _mm(a_bar, st) + jnp.where(first, _mm(a_ak[0], v), _mm(a_ak[1], v))
            xs = []
            for e in range(2):
                x, ap = rhs, a_ab[e]
                span = 1
                while True:
                    x = x + _mm(ap, x)
                    span *= 2
                    if span >= c:
                        break
                    ap = _mm(ap, ap)
                xs.append(x)
            u = jnp.where(first, xs[0], xs[1])

            y = _mm(r_bar, st) + jnp.where(first, _mm(r_b[0], u) + _mm(r_k[0], v),
                                           _mm(r_b[1], u) + _mm(r_k[1], v))
            upd = _mm_tn(b * suffix, u) + _mm_tn(kf * suffix, v)
            st_ref[pp] = st * decay_mat + jnp.where(same_head, upd, 0.0)

            mean = head_sum(y) * (1.0 / HEAD_DIM)
            dy = y - mean
            var = head_sum(dy * dy) * (1.0 / HEAD_DIM)
            yn = dy * lax.rsqrt(var + GN_EPS) * lnw_ref[:, ln] + lnb_ref[:, ln]
            y_ref[rows, ln] = ((yn + bonus) * g_ref[rows, ln].astype(F32)).astype(y_ref.dtype)
        return carry

    lax.fori_loop(0, tb // c, chunk, 0)
    sout_ref[...] = st_ref[...]


def _rwkv7(r, k, v, wl, a, g, r_last, k_last, v_last, s0, p, pairs_per_step=4):
    b, t, ca = r.shape
    n_pairs = ca // PAIR
    pps = min(pairs_per_step, n_pairs)
    lanes = pps * PAIR
    tb = min(t, 256)
    assert t % tb == 0 and tb % min(CHUNK, tb) == 0
    seq_spec = pl.BlockSpec((None, tb, lanes), lambda ib, ig, it: (ib, it, ig))
    row_spec = pl.BlockSpec((None, 1, lanes), lambda ib, ig, it: (ib, 0, ig))
    st_spec = pl.BlockSpec((None, pps, PAIR, PAIR), lambda ib, ig, it: (ib, ig, 0, 0))
    par_spec = pl.BlockSpec((1, lanes), lambda ib, ig, it: (0, ig))
    mu = p["mu_rkv"].reshape(3, 1, ca)
    vecs = [mu[0], mu[1], mu[2], p["k_k"].reshape(1, ca), p["k_a"].reshape(1, ca), p["r_k"].reshape(1, ca),
            p["ln_x_w"].reshape(1, ca), p["ln_x_b"].reshape(1, ca)]
    return pl.pallas_call(
        _rwkv_kernel,
        out_shape=[jax.ShapeDtypeStruct((b, t, ca), BF16), jax.ShapeDtypeStruct(s0.shape, F32)],
        grid=(b, n_pairs // pps, t // tb),
        in_specs=[seq_spec] * 6 + [row_spec] * 3 + [st_spec] + [par_spec] * 8,
        out_specs=[seq_spec, st_spec],
        scratch_shapes=[pltpu.VMEM((pps, PAIR, PAIR), F32), pltpu.VMEM((8, lanes), F32)],
        compiler_params=_cparams("arbitrary", "arbitrary", "arbitrary"),
        name="rwkv7",
    )(r, k, v, wl, a, g, r_last, k_last, v_last, s0, *vecs)


def _state_to_pairs(s):
    b, h, n, _ = s.shape
    st = jnp.swapaxes(s, -1, -2).reshape(b, h // 2, 2, n, n)
    eye = jnp.eye(2, dtype=s.dtype)
    return jnp.einsum("bpejk,ef->bpejfk", st, eye).reshape(b, h // 2, 2 * n, 2 * n)


def _pairs_to_state(sp, n=HEAD_DIM):
    b, hp = sp.shape[:2]
    s6 = sp.reshape(b, hp, 2, n, 2, n)
    st = jnp.stack([s6[:, :, 0, :, 0, :], s6[:, :, 1, :, 1, :]], axis=2)
    return jnp.swapaxes(st, -1, -2).reshape(b, hp * 2, n, n)


def _matmul_kernel(a_ref, w_ref, o_ref):
    o_ref[...] = _dot(a_ref[...].astype(BF16), w_ref[...])


def _matmul_rows(a, w_bf16, tn=1024):
    m, kd = a.shape
    n = w_bf16.shape[1]
    assert n % tn == 0
    return pl.pallas_call(
        _matmul_kernel,
        out_shape=jax.ShapeDtypeStruct((m, n), F32),
        grid=(n // tn,),
        in_specs=[pl.BlockSpec((m, kd), lambda j: (0, 0)), pl.BlockSpec((kd, tn), lambda j: (0, j))],
        out_specs=pl.BlockSpec((m, tn), lambda j: (0, j)),
        compiler_params=_cparams("arbitrary"),
        name="matmul_rows",
    )(a, w_bf16)


ROUTER_ROWS = 40


def _rms(x, g):
    return x * lax.rsqrt(jnp.mean(x * x, axis=-1, keepdims=True) + RMS_EPS) * g


def _postmix_kernel(ya_ref, yb_ref, x_ref, gt_ref, sc_ref, sh_ref, gpost_ref, gpre_ref,
                    woa_ref, wob_ref, wrh_ref, wrl_ref, br_ref,
                    x1_ref, h2_ref, eid_ref, wt_ref):
    bb, tt, d = x_ref.shape
    rows = bb * tt
    o = (_dot(ya_ref[...].reshape(rows, ya_ref.shape[-1]), woa_ref[...])
         + _dot(yb_ref[...].reshape(rows, yb_ref.shape[-1]), wob_ref[...])).reshape(bb, tt, d)
    x1 = x_ref[...] + gt_ref[...] * _rms(o, gpost_ref[...])
    x1_ref[...] = x1
    h2 = _mod_norm(x1, gpre_ref[...], sc_ref[...], sh_ref[...]).reshape(rows, d)
    h2_ref[...] = h2

    hi = h2.astype(BF16)
    lo = (h2 - hi.astype(F32)).astype(BF16)
    logits = (_dot_nt(wrh_ref[...], hi) + _dot_nt(wrh_ref[...], lo) + _dot_nt(wrl_ref[...], hi)) + br_ref[...]
    le = logits[0:N_EXPERTS, :]
    lg = logits[N_EXPERTS:N_EXPERTS + N_GROUPS, :]
    gio = lax.broadcasted_iota(I32, lg.shape, 0)
    gmax = jnp.max(lg, axis=0, keepdims=True)
    gi = jnp.min(jnp.where(lg == gmax, gio, N_GROUPS), axis=0, keepdims=True)
    pg = 1.0 / jnp.sum(jnp.exp(lg - gmax), axis=0, keepdims=True)
    eio = lax.broadcasted_iota(I32, le.shape, 0)
    le1 = jnp.where(eio // EXPERTS_PER_GROUP == gi, le, NEG_BIG)
    m1 = jnp.max(le1, axis=0, keepdims=True)
    i1 = jnp.min(jnp.where(le1 == m1, eio, N_EXPERTS), axis=0, keepdims=True)
    le2 = jnp.where(eio == i1, NEG_BIG, le1)
    m2 = jnp.max(le2, axis=0, keepdims=True)
    i2 = jnp.min(jnp.where(le2 == m2, eio, N_EXPERTS), axis=0, keepdims=True)
    e2 = jnp.exp(m2 - m1)
    den = 1.0 + e2
    eid_ref[0:1, :] = i1
    eid_ref[1:2, :] = i2
    wt_ref[0:1, :] = pg / den
    wt_ref[1:2, :] = pg * e2 / den


def _postmix(ya, yb, x, gt, sc, sh, g_post, g_pre, wo_a, wo_b, wr_hi, wr_lo, br):
    b, t, d = x.shape
    bb, tt = _row_tiling(b, t)
    rows = bb * tt
    nt = t // tt
    n = b * t

    def row_spec(w):
        return pl.BlockSpec((bb, tt, w), lambda ib, it: (ib, it, 0))

    mod_spec = pl.BlockSpec((bb, 1, d), lambda ib, it: (ib, 0, 0))
    vec_spec = pl.BlockSpec((1, 1, d), lambda ib, it: (0, 0, 0))

    def full(a):
        return pl.BlockSpec(a.shape, lambda ib, it: (0,) * a.ndim)

    tok_spec = pl.BlockSpec((TOP_K, rows), lambda ib, it: (0, ib * nt + it))
    weights = [wo_a, wo_b, wr_hi, wr_lo, br]
    return pl.pallas_call(
        _postmix_kernel,
        out_shape=[jax.ShapeDtypeStruct((b, t, d), F32), jax.ShapeDtypeStruct((n, d), F32),
                   jax.ShapeDtypeStruct((TOP_K, n), I32), jax.ShapeDtypeStruct((TOP_K, n), F32)],
        grid=(b // bb, nt),
        in_specs=[row_spec(ya.shape[-1]), row_spec(yb.shape[-1]), row_spec(d), mod_spec, mod_spec, mod_spec,
                  vec_spec, vec_spec] + [full(a) for a in weights],
        out_specs=[row_spec(d), pl.BlockSpec((rows, d), lambda ib, it: (ib * nt + it, 0)), tok_spec, tok_spec],
        compiler_params=_cparams("arbitrary", "arbitrary"),
        name="postmix",
    )(ya, yb, x, gt, sc, sh, g_post.reshape(1, 1, d), g_pre.reshape(1, 1, d), *weights)


def _rank_kernel(eid_ref, rank_ref, cnt_ref, carry_ref):
    i = pl.program_id(0)
    tr = eid_ref.shape[1]

    @pl.when(i == 0)
    def _():
        carry_ref[...] = jnp.zeros_like(carry_ref)

    r = lax.broadcasted_iota(I32, (tr, tr), 0)
    c = lax.broadcasted_iota(I32, (tr, tr), 1)
    upper = jnp.where(r <= c, 1.0, 0.0).astype(BF16)
    eio = lax.broadcasted_iota(I32, (N_EXPERTS, tr), 0)
    base = carry_ref[...]
    for k in range(TOP_K):
        onehot = jnp.where(eio == eid_ref[k:k + 1, :], 1.0, 0.0)
        cum = _dot(onehot.astype(BF16), upper)
        rank = jnp.sum(onehot * (base + cum - onehot), axis=0, keepdims=True)
        rank_ref[k:k + 1, :] = rank.astype(I32)
        base = base + cum[:, tr - 1:tr]
    carry_ref[...] = base
    cnt_ref[...] = jnp.broadcast_to(base, cnt_ref.shape)


def _moe_rank(eid):
    n = eid.shape[1]
    tr = min(n, 512)
    assert n % tr == 0
    return pl.pallas_call(
        _rank_kernel,
        out_shape=[jax.ShapeDtypeStruct((TOP_K, n), I32), jax.ShapeDtypeStruct((N_EXPERTS, LANES), F32)],
        grid=(n // tr,),
        in_specs=[pl.BlockSpec((TOP_K, tr), lambda i: (0, i))],
        out_specs=[pl.BlockSpec((TOP_K, tr), lambda i: (0, i)),
                   pl.BlockSpec((N_EXPERTS, LANES), lambda i: (0, 0))],
        scratch_shapes=[pltpu.VMEM((N_EXPERTS, 1), F32)],
        compiler_params=_cparams("arbitrary"),
        name="moe_rank",
    )(eid)


def _dispatch_kernel(ps_ref, eid_ref, rank_ref, h_ref, xin0_ref, xin_ref, sem):
    del xin0_ref
    tt = h_ref.shape[0]

    def row_copy(t, dest):
        return pltpu.make_async_copy(h_ref.at[pl.ds(t, 1)], xin_ref.at[pl.ds(dest, 1)], sem)

    def issue(t, carry):
        for k in range(TOP_K):
            row_copy(t, ps_ref[eid_ref[k, t]] + rank_ref[k, t]).start()
        return carry

    def drain(t, carry):
        for k in range(TOP_K):
            row_copy(0, 0).wait()
        return carry

    lax.fori_loop(0, tt, issue, 0)
    lax.fori_loop(0, tt, drain, 0)


def _moe_dispatch(pad_start, eid, rank, h2, n_rows):
    n, d = h2.shape
    tt = min(n, MOE_BLOCK)
    smem_spec = pl.BlockSpec((TOP_K, tt), lambda i, ps: (0, i), memory_space=pltpu.SMEM)
    grid_spec = pltpu.PrefetchScalarGridSpec(
        num_scalar_prefetch=1,
        grid=(n // tt,),
        in_specs=[smem_spec, smem_spec,
                  pl.BlockSpec((tt, d), lambda i, ps: (i, 0)),
                  pl.BlockSpec(memory_space=pl.ANY)],
        out_specs=pl.BlockSpec(memory_space=pl.ANY),
        scratch_shapes=[pltpu.SemaphoreType.DMA(())],
    )
    return pl.pallas_call(
        _dispatch_kernel,
        out_shape=jax.ShapeDtypeStruct((n_rows, d), h2.dtype),
        grid_spec=grid_spec,
        input_output_aliases={4: 0},
        compiler_params=_cparams("arbitrary"),
        name="moe_dispatch",
    )(pad_start, eid, rank, h2, jnp.zeros((n_rows, d), h2.dtype))


def _expert_kernel(be_ref, nu_ref, x_ref, w13_ref, w2_ref, y_ref):
    del be_ref
    i = pl.program_id(0)
    f = w2_ref.shape[0]

    @pl.when(i < nu_ref[0])
    def _():
        h13 = _dot(x_ref[...].astype(BF16), w13_ref[...])
        h1 = h13[:, :f]
        hb = (h1 * _sigmoid(h1)) * h13[:, f:]
        y_ref[...] = _dot(hb.astype(BF16), w2_ref[...])

    @pl.when(i >= nu_ref[0])
    def _():
        y_ref[...] = jnp.zeros_like(y_ref)


def _moe_experts(blk_eid, n_used, xin, w13, w2):
    n_rows, d = xin.shape
    e, f, _ = w2.shape
    grid_spec = pltpu.PrefetchScalarGridSpec(
        num_scalar_prefetch=2,
        grid=(n_rows // MOE_BLOCK,),
        in_specs=[pl.BlockSpec((MOE_BLOCK, d), lambda i, be, nu: (i, 0)),
                  pl.BlockSpec((None, d, 2 * f), lambda i, be, nu: (be[i], 0, 0)),
                  pl.BlockSpec((None, f, d), lambda i, be, nu: (be[i], 0, 0))],
        out_specs=pl.BlockSpec((MOE_BLOCK, d), lambda i, be, nu: (i, 0)),
    )
    return pl.pallas_call(
        _expert_kernel,
        out_shape=jax.ShapeDtypeStruct((n_rows, d), F32),
        grid_spec=grid_spec,
        compiler_params=_cparams("arbitrary"),
        name="moe_experts",
    )(blk_eid, n_used, xin, w13, w2)


def _combine_kernel(ps_ref, eid_ref, rank_ref, wt_ref, y_ref, x1_ref, gt_ref, g_ref, o_ref, buf_ref, sem):
    bb, tt, d = x1_ref.shape
    rows = bb * tt

    def row_copy(k, t, src):
        return pltpu.make_async_copy(y_ref.at[pl.ds(src, 1)], buf_ref.at[k, pl.ds(t, 1)], sem)

    def issue(t, carry):
        for k in range(TOP_K):
            row_copy(k, t, ps_ref[eid_ref[k, t]] + rank_ref[k, t]).start()
        return carry

    def drain(t, carry):
        for k in range(TOP_K):
            row_copy(k, 0, 0).wait()
        return carry

    lax.fori_loop(0, rows, issue, 0)
    lax.fori_loop(0, rows, drain, 0)
    wt = wt_ref[...]
    f = (buf_ref[0] * wt[:, 0:1] + buf_ref[1] * wt[:, 1:2]).reshape(bb, tt, d)
    o_ref[...] = x1_ref[...] + gt_ref[...] * _rms(f, g_ref[...])


def _moe_combine(pad_start, eid, rank, wt_rows, y, x1, gt, g_post):
    b, t, d = x1.shape
    if t >= MOE_BLOCK:
        bb, tt = 1, MOE_BLOCK
    else:
        bb, tt = MOE_BLOCK // t, t
    assert t % tt == 0 and b % bb == 0
    rows = bb * tt
    nt = t // tt
    smem_spec = pl.BlockSpec((TOP_K, rows), lambda ib, it, ps: (0, ib * nt + it), memory_space=pltpu.SMEM)
    grid_spec = pltpu.PrefetchScalarGridSpec(
        num_scalar_prefetch=1,
        grid=(b // bb, nt),
        in_specs=[smem_spec, smem_spec,
                  pl.BlockSpec((rows, TOP_K), lambda ib, it, ps: (ib * nt + it, 0)),
                  pl.BlockSpec(memory_space=pl.ANY),
                  pl.BlockSpec((bb, tt, d), lambda ib, it, ps: (ib, it, 0)),
                  pl.BlockSpec((bb, 1, d), lambda ib, it, ps: (ib, 0, 0)),
                  pl.BlockSpec((1, 1, d), lambda ib, it, ps: (0, 0, 0))],
        out_specs=pl.BlockSpec((bb, tt, d), lambda ib, it, ps: (ib, it, 0)),
        scratch_shapes=[pltpu.VMEM((TOP_K, rows, d), F32), pltpu.SemaphoreType.DMA(())],
    )
    return pl.pallas_call(
        _combine_kernel,
        out_shape=jax.ShapeDtypeStruct((b, t, d), F32),
        grid_spec=grid_spec,
        compiler_params=_cparams("arbitrary", "arbitrary"),
        name="moe_combine",
    )(pad_start, eid, rank, wt_rows, y, x1, gt, g_post.reshape(1, 1, d))


def _hier_moe(h2, eid, wt, x1, gt_f, g_post, w13, w2):
    n = h2.shape[0]
    rank, cnt = _moe_rank(eid)
    counts = cnt[:, 0].astype(I32)
    padded = (counts + MOE_BLOCK - 1) // MOE_BLOCK * MOE_BLOCK
    pad_end = jnp.cumsum(padded)
    pad_start = (pad_end - padded).astype(I32)
    n_blocks = (n * TOP_K + N_EXPERTS * (MOE_BLOCK - 1) + MOE_BLOCK - 1) // MOE_BLOCK
    blk_eid = jnp.minimum(jnp.searchsorted(pad_end, jnp.arange(n_blocks, dtype=I32) * MOE_BLOCK, side="right"),
                          N_EXPERTS - 1).astype(I32)
    n_used = (pad_end[-1:] // MOE_BLOCK).astype(I32)
    xin = _moe_dispatch(pad_start, eid, rank, h2, n_blocks * MOE_BLOCK)
    y = _moe_experts(blk_eid, n_used, xin, w13, w2)
    return _moe_combine(pad_start, eid, rank, jnp.swapaxes(wt, 0, 1), y, x1, gt_f, g_post)


def _layer(x, mods, shift_prev, rkv_last, s0, cache, p, w):
    b, t, d = x.shape
    ca = p["w0"].shape[-1]
    sh_a, sc_a, gt_a, sh_f, sc_f, gt_f = mods
    r, k, v, q, kb, vb = _in_proj(x, sc_a, sh_a, p["g_pre_mix"], w["w_main"], [F32, F32, F32, BF16, F32, F32])
    wl, a, gate, logf, shift_new = _lora_heads(x, sc_a, sh_a, p["g_pre_mix"], shift_prev, p)
    ya, s_new = _rwkv7(r, k, v, wl, a, gate, rkv_last[..., :ca], rkv_last[..., ca:2 * ca], rkv_last[..., 2 * ca:],
                       s0, p)
    if cache is None:
        fq, fk = _forget_layouts(logf, None)
        yb = _fox_prompt(q, kb, vb, fq, fk)
    else:
        k_past, v_past, logf_past = cache
        past = k_past.shape[1]
        fq, fk = _forget_layouts(logf, logf_past)
        yb = _fox_sample(q, kb, vb, k_past.reshape(b, past, -1), v_past.reshape(b, past, -1), fq, fk)
    x1, h2, eid, wt = _postmix(ya, yb, x, gt_a, sc_f, sh_f, p["g_post_mix"], p["g_pre_ffn"],
                               w["wo_a"], w["wo_b"], w["wr_hi"], w["wr_lo"], w["br"])
    out = _hier_moe(h2, eid, wt, x1, gt_f, p["g_post_ffn"], w["w13"], w["w2"])
    nh = logf.shape[-1]
    return (out, shift_new.reshape(b, d), _pairs_to_state(s_new),
            kb.reshape(b, t, nh, HEAD_DIM), vb.reshape(b, t, nh, HEAD_DIM), logf)


def kernel(x_prompt, x_sample, state_shift, state_wkv, cache_k, cache_v, cache_logf, c_prompt, c_sample, w_ada, b_ada, g_pre_mix, g_post_mix, g_pre_ffn, g_post_ffn, w_in, b_f, mu_rkv, mu_wag, w0, w_lora_a, w_lora_b, a0, a_lora_a, a_lora_b, g_lora_a, g_lora_b, k_k, k_a, r_k, ln_x_w, ln_x_b, w_out, w_rg, b_rg, w_re, b_re, w1, w3, w2):
    stacked = dict(w_ada=w_ada, b_ada=b_ada, g_pre_mix=g_pre_mix, g_post_mix=g_post_mix, g_pre_ffn=g_pre_ffn,
                   g_post_ffn=g_post_ffn, w_in=w_in, b_f=b_f, mu_rkv=mu_rkv, mu_wag=mu_wag, w0=w0,
                   w_lora_a=w_lora_a, w_lora_b=w_lora_b, a0=a0, a_lora_a=a_lora_a, a_lora_b=a_lora_b,
                   g_lora_a=g_lora_a, g_lora_b=g_lora_b, k_k=k_k, k_a=k_a, r_k=r_k, ln_x_w=ln_x_w,
                   ln_x_b=ln_x_b, w_out=w_out, w_rg=w_rg, b_rg=b_rg, w_re=w_re, b_re=b_re, w1=w1, w3=w3, w2=w2)
    depth = w_ada.shape[0]
    assert depth == 1, "one layer per call"
    p = {name: arr[0] for name, arr in stacked.items()}
    bp, _, d = x_prompt.shape
    bs = x_sample.shape[0]
    ca = p["w0"].shape[-1]
    n_main = 3 * ca + 3 * (d - ca)
    p["w_forget"] = p["w_in"][:, n_main:]

    wr = _pad_to(jnp.concatenate([p["w_re"], p["w_rg"]], axis=1).T, 0, ROUTER_ROWS)
    wr_hi = wr.astype(BF16)
    w = dict(
        w_main=p["w_in"][:, :n_main].astype(BF16),
        wo_a=p["w_out"][:ca].astype(BF16), wo_b=p["w_out"][ca:].astype(BF16),
        wr_hi=wr_hi, wr_lo=(wr - wr_hi.astype(F32)).astype(BF16),
        br=_pad_to(jnp.concatenate([p["b_re"], p["b_rg"]]), 0, ROUTER_ROWS).reshape(ROUTER_ROWS, 1),
        w13=jnp.concatenate([p["w1"], p["w3"]], axis=-1).astype(BF16), w2=p["w2"].astype(BF16),
    )

    mod = _ada_mod(jnp.concatenate([c_prompt, c_sample], axis=0), p["w_ada"], p["b_ada"])
    mods = [mod[:, None, i * d:(i + 1) * d] for i in range(6)]
    mods_p = [m[:bp] for m in mods]
    mods_s = [m[bp:] for m in mods]

    n_pairs = ca // PAIR
    out_p = _layer(x_prompt, mods_p, jnp.zeros((bp, 1, d), F32), jnp.zeros((bp, 1, 3 * ca), F32),
                   jnp.zeros((bp, n_pairs, PAIR, PAIR), F32), None, p, w)
    rkv_last = _matmul_rows(state_shift[0], w["w_main"][:, :3 * ca], tn=ca)[:, None, :]
    out_s = _layer(x_sample, mods_s, state_shift[0][:, None, :], rkv_last, _state_to_pairs(state_wkv[0]),
                   (cache_k[0], cache_v[0], cache_logf[0]), p, w)
    yp, shp, wkvp, kp, vp, lfp = out_p
    ys, shs, wkvs, ks_, vs_, lfs = out_s
    return (yp, ys, shp[None], wkvp[None], kp[None], vp[None], lfp[None],
            shs[None], wkvs[None], ks_[None], vs_[None], lfs[None])
```

```python
import functools
import math

import jax
import jax.numpy as jnp
from jax import lax
from jax.experimental import pallas as pl
from jax.experimental.pallas import tpu as pltpu

F32 = jnp.float32
BF16 = jnp.bfloat16
I32 = jnp.int32

HEAD_DIM = 64
LANES = 128
PAIR = 2 * HEAD_DIM
RMS_EPS = 1e-6
GN_EPS = 64e-5
N_GROUPS = 4
EXPERTS_PER_GROUP = 8
N_EXPERTS = N_GROUPS * EXPERTS_PER_GROUP
TOP_K = 2
CHUNK = 64
VMEM_LIMIT_BYTES = 56 * 1024 * 1024
ROW_TILE = 512
MOE_BLOCK = 256


def _cparams(*sem):
    return pltpu.CompilerParams(dimension_semantics=sem, vmem_limit_bytes=VMEM_LIMIT_BYTES)


def _dot(a, b):
    return jnp.dot(a, b, preferred_element_type=F32)


def _dot_nt(a, b):
    return lax.dot_general(a, b, (((1,), (1,)), ((), ())), preferred_element_type=F32)


def _dot_tn(a, b):
    return lax.dot_general(a, b, (((0,), (0,)), ((), ())), preferred_element_type=F32)


def _split3(x):
    hi = x.astype(BF16)
    r1 = x - hi.astype(F32)
    mid = r1.astype(BF16)
    lo = (r1 - mid.astype(F32)).astype(BF16)
    return hi, mid, lo


def _dot3_exact_rhs(x, m):
    hi, mid, lo = _split3(x)
    return _dot(hi, m) + _dot(mid, m) + _dot(lo, m)


def _sigmoid(x):
    return 1.0 / (1.0 + jnp.exp(-x))


def _log_sigmoid(x):
    return jnp.minimum(x, 0.0) - jnp.log(1.0 + jnp.exp(-jnp.abs(x)))


def _mod_norm(x, g, scale, shift):
    y = x * lax.rsqrt(jnp.mean(x * x, axis=-1, keepdims=True) + RMS_EPS)
    return (y * g) * (1.0 + scale) + shift


def _row_tiling(batch, seq):
    if seq >= ROW_TILE:
        assert seq % ROW_TILE == 0
        return 1, ROW_TILE
    assert ROW_TILE % seq == 0 and batch % (ROW_TILE // seq) == 0
    return ROW_TILE // seq, seq


def _ada_kernel(c_ref, w_ref, b_ref, o_ref):
    c = c_ref[...]
    s = c * _sigmoid(c)
    o_ref[...] = _dot(s.astype(BF16), w_ref[...].astype(BF16)) + b_ref[...]


def _ada_mod(c_all, w_ada, b_ada):
    m, d = c_all.shape
    n = w_ada.shape[1]
    tn = 1024
    return pl.pallas_call(
        _ada_kernel,
        out_shape=jax.ShapeDtypeStruct((m, n), F32),
        grid=(n // tn,),
        in_specs=[pl.BlockSpec((m, d), lambda j: (0, 0)),
                  pl.BlockSpec((d, tn), lambda j: (0, j)),
                  pl.BlockSpec((1, tn), lambda j: (0, j))],
        out_specs=pl.BlockSpec((m, tn), lambda j: (0, j)),
        compiler_params=_cparams("arbitrary"),
        name="ada_mod",
    )(c_all, w_ada, b_ada.reshape(1, n))


def _proj_kernel(x_ref, sc_ref, sh_ref, g_ref, w_ref, *refs):
    outs, hs_ref = refs[:-1], refs[-1]
    j = pl.program_id(2)
    bb, tt, d = x_ref.shape

    @pl.when(j == 0)
    def _():
        h = _mod_norm(x_ref[...], g_ref[...], sc_ref[...], sh_ref[...])
        hs_ref[...] = h.reshape(bb * tt, d).astype(BF16)

    acc = _dot(hs_ref[...], w_ref[...])
    for idx, o_ref in enumerate(outs):
        @pl.when(j == idx)
        def _(o_ref=o_ref):
            o_ref[...] = acc.reshape(o_ref.shape).astype(o_ref.dtype)


def _in_proj(x, sc, sh, g, w_main, out_dtypes):
    b, t, d = x.shape
    n_out = len(out_dtypes)
    cw = w_main.shape[1] // n_out
    bb, tt = _row_tiling(b, t)
    row_spec = pl.BlockSpec((bb, tt, d), lambda ib, it, j: (ib, it, 0))
    mod_spec = pl.BlockSpec((bb, 1, d), lambda ib, it, j: (ib, 0, 0))
    out_spec = pl.BlockSpec((bb, tt, cw), lambda ib, it, j: (ib, it, 0))
    return pl.pallas_call(
        _proj_kernel,
        out_shape=[jax.ShapeDtypeStruct((b, t, cw), dt) for dt in out_dtypes],
        grid=(b // bb, t // tt, n_out),
        in_specs=[row_spec, mod_spec, mod_spec,
                  pl.BlockSpec((1, 1, d), lambda ib, it, j: (0, 0, 0)),
                  pl.BlockSpec((d, cw), lambda ib, it, j: (0, j))],
        out_specs=[out_spec] * n_out,
        scratch_shapes=[pltpu.VMEM((bb * tt, d), BF16)],
        compiler_params=_cparams("arbitrary", "arbitrary", "arbitrary"),
        name="in_proj",
    )(x, sc, sh, g.reshape(1, 1, d), w_main)


def _lora_kernel(x_ref, sc_ref, sh_ref, g_ref, prev_ref, muw_ref, mua_ref, mug_ref,
                 aw_ref, bw_ref, w0_ref, aa_ref, ba_ref, a0_ref, ag_ref, bg_ref, wf_ref, bf_ref,
                 wl_ref, a_ref, gate_ref, lf_ref, shn_ref, carry_ref):
    it = pl.program_id(1)
    bb, tt, d = x_ref.shape
    rows = bb * tt
    h = _mod_norm(x_ref[...], g_ref[...], sc_ref[...], sh_ref[...])

    @pl.when(it == 0)
    def _():
        carry_ref[...] = prev_ref[...]

    rolled = pltpu.roll(h.reshape(rows, d), 1, 0).reshape(bb, tt, d)
    tpos = lax.broadcasted_iota(I32, (bb, tt, d), 1)
    h_prev = jnp.where(tpos == 0, carry_ref[...], rolled)
    last = h[:, tt - 1:tt, :]
    carry_ref[...] = last
    shn_ref[...] = last

    xx = h_prev - h

    def mix(mu_ref):
        return (h + xx * mu_ref[...]).reshape(rows, d).astype(BF16)

    ca = wl_ref.shape[-1]
    lw = jnp.tanh(_dot(mix(muw_ref), aw_ref[...]))
    w = w0_ref[...] + _dot(lw.astype(BF16), bw_ref[...])
    wl_ref[...] = (-math.exp(-0.5) * _sigmoid(w)).reshape(bb, tt, ca)
    la = _dot(mix(mua_ref), aa_ref[...])
    a_ref[...] = _sigmoid(a0_ref[...] + _dot(la.astype(BF16), ba_ref[...])).reshape(bb, tt, ca)
    lg = _sigmoid(_dot(mix(mug_ref), ag_ref[...]))
    gate_ref[...] = _dot(lg.astype(BF16), bg_ref[...]).reshape(bb, tt, ca).astype(gate_ref.dtype)
    nh = lf_ref.shape[-1]
    z = _dot(h.reshape(rows, d).astype(BF16), wf_ref[...]) + bf_ref[...]
    lf_ref[...] = _log_sigmoid(z)[:, :nh].reshape(bb, tt, nh)


def _pad_to(x, axis, size):
    pad = [(0, 0)] * x.ndim
    pad[axis] = (0, size - x.shape[axis])
    return jnp.pad(x, pad)


def _round_up(n, m):
    return (n + m - 1) // m * m


def _lora_heads(x, sc, sh, g, shift_prev, p):
    b, t, d = x.shape
    ca = p["w0"].shape[-1]
    nh = p["b_f"].shape[-1]
    bb, tt = _row_tiling(b, t)

    def lora_pair(a_w, b_w):
        r = _round_up(a_w.shape[1], LANES)
        return _pad_to(a_w, 1, r).astype(BF16), _pad_to(b_w, 0, r).astype(BF16)

    aw, bw = lora_pair(p["w_lora_a"], p["w_lora_b"])
    aa, ba = lora_pair(p["a_lora_a"], p["a_lora_b"])
    ag, bg = lora_pair(p["g_lora_a"], p["g_lora_b"])
    wf = _pad_to(p["w_forget"], 1, LANES).astype(BF16)
    bf = _pad_to(p["b_f"].reshape(1, nh), 1, LANES)
    mu = p["mu_wag"].reshape(3, 1, 1, d)

    row_spec = pl.BlockSpec((bb, tt, d), lambda ib, it: (ib, it, 0))
    mod_spec = pl.BlockSpec((bb, 1, d), lambda ib, it: (ib, 0, 0))
    vec_spec = pl.BlockSpec((1, 1, d), lambda ib, it: (0, 0, 0))

    def full(a):
        return pl.BlockSpec(a.shape, lambda ib, it: (0,) * a.ndim)

    ca_spec = pl.BlockSpec((bb, tt, ca), lambda ib, it: (ib, it, 0))
    weights = [aw, bw, p["w0"].reshape(1, ca), aa, ba, p["a0"].reshape(1, ca), ag, bg, wf, bf]
    return pl.pallas_call(
        _lora_kernel,
        out_shape=[jax.ShapeDtypeStruct((b, t, ca), F32), jax.ShapeDtypeStruct((b, t, ca), F32),
                   jax.ShapeDtypeStruct((b, t, ca), BF16), jax.ShapeDtypeStruct((b, t, nh), F32),
                   jax.ShapeDtypeStruct((b, 1, d), F32)],
        grid=(b // bb, t // tt),
        in_specs=[row_spec, mod_spec, mod_spec, vec_spec, mod_spec, vec_spec, vec_spec, vec_spec]
                 + [full(a) for a in weights],
        out_specs=[ca_spec, ca_spec, ca_spec,
                   pl.BlockSpec((bb, tt, nh), lambda ib, it: (ib, it, 0)),
                   pl.BlockSpec((bb, 1, d), lambda ib, it: (ib, 0, 0))],
        scratch_shapes=[pltpu.VMEM((bb, 1, d), F32)],
        compiler_params=_cparams("arbitrary", "arbitrary"),
        name="lora_heads",
    )(x, sc, sh, g.reshape(1, 1, d), shift_prev, mu[0], mu[1], mu[2], *weights)


def _cumsum_kernel(x_ref, o_ref, carry_ref):
    it = pl.program_id(1)
    bb, nh, tl = x_ref.shape

    @pl.when(it == 0)
    def _():
        carry_ref[...] = jnp.zeros_like(carry_ref)

    r = lax.broadcasted_iota(I32, (tl, tl), 0)
    c = lax.broadcasted_iota(I32, (tl, tl), 1)
    upper = jnp.where(r <= c, 1.0, 0.0).astype(BF16)
    cs = _dot3_exact_rhs(x_ref[...].reshape(bb * nh, tl), upper) + carry_ref[...]
    o_ref[...] = cs.reshape(bb, nh, tl)
    carry_ref[...] = cs[:, tl - 1:tl]


def _cumsum_lanes(x):
    b, nh, l = x.shape
    tl = l if l <= 2304 else 1024
    assert l % tl == 0 and nh % 8 == 0
    bb = math.gcd(b, 8)
    return pl.pallas_call(
        _cumsum_kernel,
        out_shape=jax.ShapeDtypeStruct((b, nh, l), F32),
        grid=(b // bb, l // tl),
        in_specs=[pl.BlockSpec((bb, nh, tl), lambda ib, it: (ib, 0, it))],
        out_specs=pl.BlockSpec((bb, nh, tl), lambda ib, it: (ib, 0, it)),
        scratch_shapes=[pltpu.VMEM((bb * nh, 1), F32)],
        compiler_params=_cparams("arbitrary", "arbitrary"),
        name="logf_cumsum",
    )(x)


NEG_BIG = -1e30


def _head_masked(q):
    lane = lax.broadcasted_iota(I32, q.shape, 1)
    first = lane < HEAD_DIM
    zero = jnp.zeros_like(q)
    return first, (jnp.where(first, q, zero), jnp.where(first, zero, q))


def _fox_prompt_kernel(q_ref, k_ref, v_ref, fq_ref, fk_ref, o_ref, kbf_ref, vbf_ref):
    qi = pl.program_id(2)
    tq = q_ref.shape[0]
    tk = tq

    @pl.when(qi == 0)
    def _():
        kbf_ref[...] = k_ref[...].astype(BF16)
        vbf_ref[...] = v_ref[...].astype(BF16)

    first, qs = _head_masked(q_ref[...])
    fq = fq_ref[...]
    scale = HEAD_DIM ** -0.5
    row = lax.broadcasted_iota(I32, (tq, tk), 0)
    col = lax.broadcasted_iota(I32, (tq, tk), 1)

    def block(kj, carry, masked):
        m_prev, l_prev, acc = carry
        start = pl.multiple_of(kj * tk, tk)
        ks = kbf_ref[pl.ds(start, tk), :]
        vs = vbf_ref[pl.ds(start, tk), :]
        fk = fk_ref[:, pl.ds(start, tk)]
        m_new, l_new, alpha, pv = [], [], [], []
        for e in range(2):
            s = _dot_nt(qs[e], ks) * scale + (fq[:, e:e + 1] - fk[e:e + 1, :])
            if masked:
                s = jnp.where(col <= row, s, NEG_BIG)
            m_e = jnp.maximum(m_prev[e], jnp.max(s, axis=1, keepdims=True))
            p = jnp.exp(s - m_e)
            a_e = jnp.exp(m_prev[e] - m_e)
            m_new.append(m_e)
            alpha.append(a_e)
            l_new.append(a_e * l_prev[e] + jnp.sum(p, axis=1, keepdims=True))
            pv.append(_dot(p.astype(BF16), vs))
        acc = acc * jnp.where(first, alpha[0], alpha[1]) + jnp.where(first, pv[0], pv[1])
        return tuple(m_new), tuple(l_new), acc

    neg = jnp.full((tq, 1), NEG_BIG, F32)
    zero = jnp.zeros((tq, 1), F32)
    init = ((neg, neg), (zero, zero), jnp.zeros((tq, PAIR), F32))
    carry = lax.fori_loop(0, qi, lambda kj, c: block(kj, c, False), init)
    _, l_fin, acc = block(qi, carry, True)
    o_ref[...] = (acc / jnp.where(first, l_fin[0], l_fin[1])).astype(o_ref.dtype)


def _fox_prompt(q, k, v, fq, fk):
    b, t, cb = q.shape
    hp = cb // PAIR
    tq = min(t, 512)
    return pl.pallas_call(
        _fox_prompt_kernel,
        out_shape=jax.ShapeDtypeStruct((b, t, cb), BF16),
        grid=(b, hp, t // tq),
        in_specs=[pl.BlockSpec((None, tq, PAIR), lambda ib, ih, iq: (ib, iq, ih)),
                  pl.BlockSpec((None, t, PAIR), lambda ib, ih, iq: (ib, 0, ih)),
                  pl.BlockSpec((None, t, PAIR), lambda ib, ih, iq: (ib, 0, ih)),
                  pl.BlockSpec((None, None, tq, 2), lambda ib, ih, iq: (ib, ih, iq, 0)),
                  pl.BlockSpec((None, None, 2, t), lambda ib, ih, iq: (ib, ih, 0, 0))],
        out_specs=pl.BlockSpec((None, tq, PAIR), lambda ib, ih, iq: (ib, iq, ih)),
        scratch_shapes=[pltpu.VMEM((t, PAIR), BF16), pltpu.VMEM((t, PAIR), BF16)],
        compiler_params=_cparams("arbitrary", "arbitrary", "arbitrary"),
        name="fox_prompt",
    )(q, k, v, fq, fk)


def _fox_sample_kernel(q_ref, kn_ref, vn_ref, kc_ref, vc_ref, fq_ref, fk_ref, o_ref):
    ts = q_ref.shape[0]
    past = kc_ref.shape[0]
    n_pairs = q_ref.shape[1] // PAIR
    scale = HEAD_DIM ** -0.5
    row = lax.broadcasted_iota(I32, (ts, ts), 0)
    col = lax.broadcasted_iota(I32, (ts, ts), 1)
    for pp in range(n_pairs):
        lanes = slice(pp * PAIR, (pp + 1) * PAIR)
        first, qs = _head_masked(q_ref[:, lanes])
        kc = kc_ref[:, lanes].astype(BF16)
        vc = vc_ref[:, lanes].astype(BF16)
        kn = kn_ref[:, lanes].astype(BF16)
        vn = vn_ref[:, lanes].astype(BF16)
        outs = []
        for e in range(2):
            fq = fq_ref[pp, :, e:e + 1]
            s_p = _dot_nt(qs[e], kc) * scale + (fq - fk_ref[pp, e:e + 1, 0:past])
            s_n = _dot_nt(qs[e], kn) * scale + (fq - fk_ref[pp, e:e + 1, past:past + ts])
            s_n = jnp.where(col <= row, s_n, NEG_BIG)
            m = jnp.maximum(jnp.max(s_p, axis=1, keepdims=True), jnp.max(s_n, axis=1, keepdims=True))
            p_p = jnp.exp(s_p - m)
            p_n = jnp.exp(s_n - m)
            l = jnp.sum(p_p, axis=1, keepdims=True) + jnp.sum(p_n, axis=1, keepdims=True)
            outs.append((_dot(p_p.astype(BF16), vc) + _dot(p_n.astype(BF16), vn)) / l)
        o_ref[:, lanes] = jnp.where(first, outs[0], outs[1]).astype(o_ref.dtype)


def _fox_sample(q, kn, vn, kc, vc, fq, fk):
    b, ts, cb = q.shape
    past = kc.shape[1]
    lp = fk.shape[-1]
    lb = min(cb, 4 * PAIR)
    npb = lb // PAIR
    new_spec = pl.BlockSpec((None, ts, lb), lambda ib, ig: (ib, 0, ig))
    past_spec = pl.BlockSpec((None, past, lb), lambda ib, ig: (ib, 0, ig))
    return pl.pallas_call(
        _fox_sample_kernel,
        out_shape=jax.ShapeDtypeStruct((b, ts, cb), BF16),
        grid=(b, cb // lb),
        in_specs=[new_spec, new_spec, new_spec, past_spec, past_spec,
                  pl.BlockSpec((None, npb, ts, 2), lambda ib, ig: (ib, ig, 0, 0)),
                  pl.BlockSpec((None, npb, 2, lp), lambda ib, ig: (ib, ig, 0, 0))],
        out_specs=new_spec,
        compiler_params=_cparams("arbitrary", "arbitrary"),
        name="fox_sample",
    )(q, kn, vn, kc, vc, fq, fk)


def _forget_layouts(logf_new, logf_past):
    b, t, nh = logf_new.shape
    lt = jnp.swapaxes(logf_new, 1, 2)
    past = 0
    if logf_past is not None:
        past = logf_past.shape[1]
        lt = jnp.concatenate([jnp.swapaxes(logf_past, 1, 2), lt], axis=2)
    lp = _round_up(past + t, LANES)
    ft = _cumsum_lanes(_pad_to(lt, 2, lp))
    fk = ft.reshape(b, nh // 2, 2, lp)
    fq = jnp.swapaxes(fk[..., past:past + t], 2, 3)
    return fq, fk


def _split2(x):
    hi = x.astype(BF16)
    return hi, (x - hi.astype(F32)).astype(BF16)


def _rwkv_kernel(r_ref, k_ref, v_ref, wl_ref, a_ref, g_ref, rl_ref, kl_ref, vl_ref, s0_ref,
                 mur_ref, muk_ref, muv_ref, kk_ref, ka_ref, rk_ref, lnw_ref, lnb_ref,
                 y_ref, sout_ref, st_ref, prev_ref):
    itb = pl.program_id(2)
    tb, lanes = r_ref.shape
    n_pairs = lanes // PAIR
    c = min(CHUNK, tb)

    @pl.when(itb == 0)
    def _():
        st_ref[...] = s0_ref[...]
        prev_ref[0:1, :] = rl_ref[...]
        prev_ref[1:2, :] = kl_ref[...]
        prev_ref[2:3, :] = vl_ref[...]

    pairs = range(n_pairs)
    heads = range(2)
    ri = lax.broadcasted_iota(I32, (c, 2 * c), 0)
    ci = lax.broadcasted_iota(I32, (c, 2 * c), 1) % c
    strict2 = ri > ci
    incl2 = ri >= ci
    tri_ones = jnp.where(incl2[:, :c], 1.0, 0.0).astype(BF16)
    li = lax.broadcasted_iota(I32, (PAIR, PAIR), 0) // HEAD_DIM
    lj = lax.broadcasted_iota(I32, (PAIR, PAIR), 1) // HEAD_DIM
    same_head = li == lj
    head_ones = jnp.where(same_head, 1.0, 0.0).astype(BF16)
    first = lax.broadcasted_iota(I32, (c, PAIR), 1) < HEAD_DIM
    row0 = lax.broadcasted_iota(I32, (c, PAIR), 0) == 0
    zeros_cb = jnp.zeros((c, PAIR), BF16)

    def head_sum2(x):
        hi, lo = _split2(x)
        s = _dot(jnp.concatenate([hi, lo], axis=0), head_ones)
        return s[:c] + s[c:]

    def shifted(x, prev_row):
        return jnp.where(row0, prev_row, pltpu.roll(x, 1, 0))

    def by_head(x):
        return [jnp.where(first, x, zeros_cb), jnp.where(first, zeros_cb, x)]

    def chunk(ic, carry):
        rows = pl.ds(pl.multiple_of(ic * c, c), c)
        lns = [slice(pp * PAIR, (pp + 1) * PAIR) for pp in pairs]
        r, k, v, wl, a, kk = [], [], [], [], [], []
        for ln in lns:
            r_raw, k_raw, v_raw = r_ref[rows, ln], k_ref[rows, ln], v_ref[rows, ln]
            r.append(r_raw + mur_ref[:, ln] * (shifted(r_raw, prev_ref[0:1, ln]) - r_raw))
            k.append(k_raw + muk_ref[:, ln] * (shifted(k_raw, prev_ref[1:2, ln]) - k_raw))
            v.append(v_raw + muv_ref[:, ln] * (shifted(v_raw, prev_ref[2:3, ln]) - v_raw))
            prev_ref[0:1, ln] = r_raw[c - 1:c, :]
            prev_ref[1:2, ln] = k_raw[c - 1:c, :]
            prev_ref[2:3, ln] = v_raw[c - 1:c, :]
            wl.append(wl_ref[rows, ln])
            a.append(a_ref[rows, ln])
            kk.append(k[-1] * kk_ref[:, ln])
        lcum = []
        for p in pairs:
            s = _dot(tri_ones, jnp.concatenate(_split3(wl[p]), axis=1))
            lcum.append(s[:, :PAIR] + s[:, PAIR:2 * PAIR] + s[:, 2 * PAIR:])
        kk_ss = [head_sum2(x * x) for x in kk]
        kf = [k[p] * (1.0 + (a[p] - 1.0) * ka_ref[:, lns[p]]) for p in pairs]
        bonus_s = [_dot((r[p] * kf[p] * rk_ref[:, lns[p]]).astype(BF16), head_ones) for p in pairs]
        kk = [kk[p] / jnp.maximum(jnp.sqrt(kk_ss[p]), 1e-12) for p in pairs]
        b = [kk[p] * a[p] for p in pairs]
        inv = [jnp.exp(-lcum[p]) for p in pairs]
        ltot = [lcum[p][c - 1:c, :] for p in pairs]
        suffix = [jnp.exp(ltot[p] - lcum[p]) for p in pairs]
        r_bar = [(r[p] * jnp.exp(lcum[p])).astype(BF16) for p in pairs]
        a_bar = [(-kk[p] * jnp.exp(lcum[p] - wl[p])).astype(BF16) for p in pairs]
        bk_bar = [jnp.concatenate([(b[p] * inv[p]).astype(BF16), (kf[p] * inv[p]).astype(BF16)], axis=0)
                  for p in pairs]
        bk_suf = [jnp.concatenate([(b[p] * suffix[p]).astype(BF16), (kf[p] * suffix[p]).astype(BF16)], axis=0)
                  for p in pairs]
        vb = [x.astype(BF16) for x in v]
        st = [st_ref[p] for p in pairs]
        stb = [x.astype(BF16) for x in st]

        prod = [_dot_nt(jnp.concatenate(by_head(a_bar[p]) + by_head(r_bar[p]), axis=0), bk_bar[p]) for p in pairs]
        a_full = [[jnp.where(strict2, prod[p][e * c:(e + 1) * c], 0.0).astype(BF16) for e in heads] for p in pairs]
        r_full = [[jnp.where(incl2, prod[p][(2 + e) * c:(3 + e) * c], 0.0).astype(BF16) for e in heads]
                  for p in pairs]
        on_s = [_dot_nt(jnp.concatenate([a_bar[p], r_bar[p]], axis=0), stb[p]) for p in pairs]
        zv = [jnp.concatenate([zeros_cb, vb[p]], axis=0) for p in pairs]
        rhs_v = [[_dot(a_full[p][e], zv[p]) for e in heads] for p in pairs]
        x = [[on_s[p][:c] + jnp.where(first, rhs_v[p][0], rhs_v[p][1])] * 2 for p in pairs]
        x = [list(xp) for xp in x]
        ap = [[a_full[p][e][:, :c] for e in heads] for p in pairs]
        span = 1
        while True:
            span *= 2
            if span >= c:
                x = [[x[p][e] + _dot(ap[p][e], x[p][e].astype(BF16)) for e in heads] for p in pairs]
                break
            res = [[_dot(ap[p][e], jnp.concatenate([x[p][e].astype(BF16), ap[p][e]], axis=1)) for e in heads]
                   for p in pairs]
            x = [[x[p][e] + res[p][e][:, :PAIR] for e in heads] for p in pairs]
            ap = [[res[p][e][:, PAIR:].astype(BF16) for e in heads] for p in pairs]
        uv = [jnp.concatenate([jnp.where(first, x[p][0], x[p][1]).astype(BF16), vb[p]], axis=0) for p in pairs]

        y_h = [[_dot(r_full[p][e], uv[p]) for e in heads] for p in pairs]
        upd = [_dot_tn(uv[p], bk_suf[p]) for p in pairs]
        for p in pairs:
            st_ref[p] = st[p] * jnp.exp(ltot[p]) + jnp.where(same_head, upd[p], 0.0)

        y = [on_s[p][c:] + jnp.where(first, y_h[p][0], y_h[p][1]) for p in pairs]
        mean = [head_sum2(y[p]) * (1.0 / HEAD_DIM) for p in pairs]
        dy = [y[p] - mean[p] for p in pairs]
        var = [head_sum2(dy[p] * dy[p]) * (1.0 / HEAD_DIM) for p in pairs]
        for p in pairs:
            ln = lns[p]
            yn = dy[p] * lax.rsqrt(var[p] + GN_EPS) * lnw_ref[:, ln] + lnb_ref[:, ln]
            y_ref[rows, ln] = ((yn + bonus_s[p] * v[p]) * g_ref[rows, ln].astype(F32)).astype(y_ref.dtype)
        return carry

    lax.fori_loop(0, tb // c, chunk, 0)
    sout_ref[...] = st_ref[...]


def _rwkv7(r, k, v, wl, a, g, r_last, k_last, v_last, s0, p, pairs_per_step=8):
    b, t, ca = r.shape
    n_pairs = ca // PAIR
    pps = min(pairs_per_step, n_pairs)
    lanes = pps * PAIR
    tb = min(t, 256)
    assert t % tb == 0 and tb % min(CHUNK, tb) == 0
    seq_spec = pl.BlockSpec((None, tb, lanes), lambda ib, ig, it: (ib, it, ig))
    row_spec = pl.BlockSpec((None, 1, lanes), lambda ib, ig, it: (ib, 0, ig))
    st_spec = pl.BlockSpec((None, pps, PAIR, PAIR), lambda ib, ig, it: (ib, ig, 0, 0))
    par_spec = pl.BlockSpec((1, lanes), lambda ib, ig, it: (0, ig))
    mu = p["mu_rkv"].reshape(3, 1, ca)
    vecs = [mu[0], mu[1], mu[2], p["k_k"].reshape(1, ca), p["k_a"].reshape(1, ca), p["r_k"].reshape(1, ca),
            p["ln_x_w"].reshape(1, ca), p["ln_x_b"].reshape(1, ca)]
    return pl.pallas_call(
        _rwkv_kernel,
        out_shape=[jax.ShapeDtypeStruct((b, t, ca), BF16), jax.ShapeDtypeStruct(s0.shape, F32)],
        grid=(b, n_pairs // pps, t // tb),
        in_specs=[seq_spec] * 6 + [row_spec] * 3 + [st_spec] + [par_spec] * 8,
        out_specs=[seq_spec, st_spec],
        scratch_shapes=[pltpu.VMEM((pps, PAIR, PAIR), F32), pltpu.VMEM((8, lanes), F32)],
        compiler_params=_cparams("arbitrary", "arbitrary", "arbitrary"),
        name="rwkv7",
    )(r, k, v, wl, a, g, r_last, k_last, v_last, s0, *vecs)


def _state_to_pairs(s):
    b, h, n, _ = s.shape
    eye = jnp.eye(2, dtype=s.dtype)
    return jnp.einsum("bpejk,ef->bpejfk", s.reshape(b, h // 2, 2, n, n), eye).reshape(b, h // 2, 2 * n, 2 * n)


def _pairs_to_state(sp, n=HEAD_DIM):
    b, hp = sp.shape[:2]
    s6 = sp.reshape(b, hp, 2, n, 2, n)
    return jnp.stack([s6[:, :, 0, :, 0, :], s6[:, :, 1, :, 1, :]], axis=2).reshape(b, hp * 2, n, n)


def _matmul_kernel(a_ref, w_ref, o_ref):
    o_ref[...] = _dot(a_ref[...].astype(BF16), w_ref[...])


def _matmul_rows(a, w_bf16, tn=1024):
    m, kd = a.shape
    n = w_bf16.shape[1]
    assert n % tn == 0
    return pl.pallas_call(
        _matmul_kernel,
        out_shape=jax.ShapeDtypeStruct((m, n), F32),
        grid=(n // tn,),
        in_specs=[pl.BlockSpec((m, kd), lambda j: (0, 0)), pl.BlockSpec((kd, tn), lambda j: (0, j))],
        out_specs=pl.BlockSpec((m, tn), lambda j: (0, j)),
        compiler_params=_cparams("arbitrary"),
        name="matmul_rows",
    )(a, w_bf16)


ROUTER_ROWS = 40


def _rms(x, g):
    return x * lax.rsqrt(jnp.mean(x * x, axis=-1, keepdims=True) + RMS_EPS) * g


def _postmix_kernel(ya_ref, yb_ref, x_ref, gt_ref, sc_ref, sh_ref, gpost_ref, gpre_ref,
                    woa_ref, wob_ref, wrh_ref, wrl_ref, br_ref,
                    x1_ref, h2_ref, eid_ref, wt_ref):
    bb, tt, d = x_ref.shape
    rows = bb * tt
    o = (_dot(ya_ref[...].reshape(rows, ya_ref.shape[-1]), woa_ref[...])
         + _dot(yb_ref[...].reshape(rows, yb_ref.shape[-1]), wob_ref[...])).reshape(bb, tt, d)
    x1 = x_ref[...] + gt_ref[...] * _rms(o, gpost_ref[...])
    x1_ref[...] = x1
    h2 = _mod_norm(x1, gpre_ref[...], sc_ref[...], sh_ref[...]).reshape(rows, d)
    h2_ref[...] = h2

    hi = h2.astype(BF16)
    lo = (h2 - hi.astype(F32)).astype(BF16)
    logits = (_dot_nt(wrh_ref[...], hi) + _dot_nt(wrh_ref[...], lo) + _dot_nt(wrl_ref[...], hi)) + br_ref[...]
    le = logits[0:N_EXPERTS, :]
    lg = logits[N_EXPERTS:N_EXPERTS + N_GROUPS, :]
    gio = lax.broadcasted_iota(I32, lg.shape, 0)
    gmax = jnp.max(lg, axis=0, keepdims=True)
    gi = jnp.min(jnp.where(lg == gmax, gio, N_GROUPS), axis=0, keepdims=True)
    pg = 1.0 / jnp.sum(jnp.exp(lg - gmax), axis=0, keepdims=True)
    eio = lax.broadcasted_iota(I32, le.shape, 0)
    le1 = jnp.where(eio // EXPERTS_PER_GROUP == gi, le, NEG_BIG)
    m1 = jnp.max(le1, axis=0, keepdims=True)
    i1 = jnp.min(jnp.where(le1 == m1, eio, N_EXPERTS), axis=0, keepdims=True)
    le2 = jnp.where(eio == i1, NEG_BIG, le1)
    m2 = jnp.max(le2, axis=0, keepdims=True)
    i2 = jnp.min(jnp.where(le2 == m2, eio, N_EXPERTS), axis=0, keepdims=True)
    e2 = jnp.exp(m2 - m1)
    den = 1.0 + e2
    eid_ref[0:1, :] = i1
    eid_ref[1:2, :] = i2
    wt_ref[0:1, :] = pg / den
    wt_ref[1:2, :] = pg * e2 / den


def _postmix(ya, yb, x, gt, sc, sh, g_post, g_pre, wo_a, wo_b, wr_hi, wr_lo, br):
    b, t, d = x.shape
    bb, tt = _row_tiling(b, t)
    rows = bb * tt
    nt = t // tt
    n = b * t

    def row_spec(w):
        return pl.BlockSpec((bb, tt, w), lambda ib, it: (ib, it, 0))

    mod_spec = pl.BlockSpec((bb, 1, d), lambda ib, it: (ib, 0, 0))
    vec_spec = pl.BlockSpec((1, 1, d), lambda ib, it: (0, 0, 0))

    def full(a):
        return pl.BlockSpec(a.shape, lambda ib, it: (0,) * a.ndim)

    tok_spec = pl.BlockSpec((TOP_K, rows), lambda ib, it: (0, ib * nt + it))
    weights = [wo_a, wo_b, wr_hi, wr_lo, br]
    return pl.pallas_call(
        _postmix_kernel,
        out_shape=[jax.ShapeDtypeStruct((b, t, d), F32), jax.ShapeDtypeStruct((n, d), F32),
                   jax.ShapeDtypeStruct((TOP_K, n), I32), jax.ShapeDtypeStruct((TOP_K, n), F32)],
        grid=(b // bb, nt),
        in_specs=[row_spec(ya.shape[-1]), row_spec(yb.shape[-1]), row_spec(d), mod_spec, mod_spec, mod_spec,
                  vec_spec, vec_spec] + [full(a) for a in weights],
        out_specs=[row_spec(d), pl.BlockSpec((rows, d), lambda ib, it: (ib * nt + it, 0)), tok_spec, tok_spec],
        compiler_params=_cparams("arbitrary", "arbitrary"),
        name="postmix",
    )(ya, yb, x, gt, sc, sh, g_post.reshape(1, 1, d), g_pre.reshape(1, 1, d), *weights)


def _rank_kernel(eid_ref, rank_ref, cnt_ref, carry_ref):
    i = pl.program_id(0)
    tr = eid_ref.shape[1]

    @pl.when(i == 0)
    def _():
        carry_ref[...] = jnp.zeros_like(carry_ref)

    r = lax.broadcasted_iota(I32, (tr, tr), 0)
    c = lax.broadcasted_iota(I32, (tr, tr), 1)
    upper = jnp.where(r <= c, 1.0, 0.0).astype(BF16)
    eio = lax.broadcasted_iota(I32, (N_EXPERTS, tr), 0)
    base = carry_ref[...]
    for k in range(TOP_K):
        onehot = jnp.where(eio == eid_ref[k:k + 1, :], 1.0, 0.0)
        cum = _dot(onehot.astype(BF16), upper)
        rank = jnp.sum(onehot * (base + cum - onehot), axis=0, keepdims=True)
        rank_ref[k:k + 1, :] = rank.astype(I32)
        base = base + cum[:, tr - 1:tr]
    carry_ref[...] = base
    cnt_ref[...] = jnp.broadcast_to(base, cnt_ref.shape)


def _moe_rank(eid):
    n = eid.shape[1]
    tr = min(n, 512)
    assert n % tr == 0
    return pl.pallas_call(
        _rank_kernel,
        out_shape=[jax.ShapeDtypeStruct((TOP_K, n), I32), jax.ShapeDtypeStruct((N_EXPERTS, LANES), F32)],
        grid=(n // tr,),
        in_specs=[pl.BlockSpec((TOP_K, tr), lambda i: (0, i))],
        out_specs=[pl.BlockSpec((TOP_K, tr), lambda i: (0, i)),
                   pl.BlockSpec((N_EXPERTS, LANES), lambda i: (0, 0))],
        scratch_shapes=[pltpu.VMEM((N_EXPERTS, 1), F32)],
        compiler_params=_cparams("arbitrary"),
        name="moe_rank",
    )(eid)


def _dispatch_kernel(ps_ref, eid_ref, rank_ref, h_ref, xin0_ref, xin_ref, sem):
    del xin0_ref
    tt = h_ref.shape[0]

    def row_copy(t, dest):
        return pltpu.make_async_copy(h_ref.at[pl.ds(t, 1)], xin_ref.at[pl.ds(dest, 1)], sem)

    def issue(t, carry):
        for k in range(TOP_K):
            row_copy(t, ps_ref[eid_ref[k, t]] + rank_ref[k, t]).start()
        return carry

    def drain(t, carry):
        for k in range(TOP_K):
            row_copy(0, 0).wait()
        return carry

    lax.fori_loop(0, tt, issue, 0)
    lax.fori_loop(0, tt, drain, 0)


def _moe_dispatch(pad_start, eid, rank, h2, n_rows):
    n, d = h2.shape
    tt = min(n, MOE_BLOCK)
    smem_spec = pl.BlockSpec((TOP_K, tt), lambda i, ps: (0, i), memory_space=pltpu.SMEM)
    grid_spec = pltpu.PrefetchScalarGridSpec(
        num_scalar_prefetch=1,
        grid=(n // tt,),
        in_specs=[smem_spec, smem_spec,
                  pl.BlockSpec((tt, d), lambda i, ps: (i, 0)),
                  pl.BlockSpec(memory_space=pl.ANY)],
        out_specs=pl.BlockSpec(memory_space=pl.ANY),
        scratch_shapes=[pltpu.SemaphoreType.DMA(())],
    )
    return pl.pallas_call(
        _dispatch_kernel,
        out_shape=jax.ShapeDtypeStruct((n_rows, d), h2.dtype),
        grid_spec=grid_spec,
        input_output_aliases={4: 0},
        compiler_params=_cparams("arbitrary"),
        name="moe_dispatch",
    )(pad_start, eid, rank, h2, jnp.zeros((n_rows, d), h2.dtype))


def _expert_kernel(be_ref, nu_ref, x_ref, w13_ref, w2_ref, y_ref):
    del be_ref
    i = pl.program_id(0)
    f = w2_ref.shape[0]

    @pl.when(i < nu_ref[0])
    def _():
        h13 = _dot(x_ref[...].astype(BF16), w13_ref[...])
        h1 = h13[:, :f]
        hb = (h1 * _sigmoid(h1)) * h13[:, f:]
        y_ref[...] = _dot(hb.astype(BF16), w2_ref[...])

    @pl.when(i >= nu_ref[0])
    def _():
        y_ref[...] = jnp.zeros_like(y_ref)


def _moe_experts(blk_eid, n_used, xin, w13, w2):
    n_rows, d = xin.shape
    e, f, _ = w2.shape
    grid_spec = pltpu.PrefetchScalarGridSpec(
        num_scalar_prefetch=2,
        grid=(n_rows // MOE_BLOCK,),
        in_specs=[pl.BlockSpec((MOE_BLOCK, d), lambda i, be, nu: (i, 0)),
                  pl.BlockSpec((None, d, 2 * f), lambda i, be, nu: (be[i], 0, 0)),
                  pl.BlockSpec((None, f, d), lambda i, be, nu: (be[i], 0, 0))],
        out_specs=pl.BlockSpec((MOE_BLOCK, d), lambda i, be, nu: (i, 0)),
    )
    return pl.pallas_call(
        _expert_kernel,
        out_shape=jax.ShapeDtypeStruct((n_rows, d), F32),
        grid_spec=grid_spec,
        compiler_params=_cparams("arbitrary"),
        name="moe_experts",
    )(blk_eid, n_used, xin, w13, w2)


def _combine_kernel(ps_ref, eid_ref, rank_ref, wt_ref, y_ref, x1_ref, gt_ref, g_ref, o_ref, buf_ref, sem):
    bb, tt, d = x1_ref.shape
    rows = bb * tt

    def row_copy(k, t, src):
        return pltpu.make_async_copy(y_ref.at[pl.ds(src, 1)], buf_ref.at[k, pl.ds(t, 1)], sem)

    def issue(t, carry):
        for k in range(TOP_K):
            row_copy(k, t, ps_ref[eid_ref[k, t]] + rank_ref[k, t]).start()
        return carry

    def drain(t, carry):
        for k in range(TOP_K):
            row_copy(k, 0, 0).wait()
        return carry

    lax.fori_loop(0, rows, issue, 0)
    lax.fori_loop(0, rows, drain, 0)
    wt = wt_ref[...]
    f = (buf_ref[0] * wt[:, 0:1] + buf_ref[1] * wt[:, 1:2]).reshape(bb, tt, d)
    o_ref[...] = x1_ref[...] + gt_ref[...] * _rms(f, g_ref[...])


def _moe_combine(pad_start, eid, rank, wt_rows, y, x1, gt, g_post):
    b, t, d = x1.shape
    if t >= MOE_BLOCK:
        bb, tt = 1, MOE_BLOCK
    else:
        bb, tt = MOE_BLOCK // t, t
    assert t % tt == 0 and b % bb == 0
    rows = bb * tt
    nt = t // tt
    smem_spec = pl.BlockSpec((TOP_K, rows), lambda ib, it, ps: (0, ib * nt + it), memory_space=pltpu.SMEM)
    grid_spec = pltpu.PrefetchScalarGridSpec(
        num_scalar_prefetch=1,
        grid=(b // bb, nt),
        in_specs=[smem_spec, smem_spec,
                  pl.BlockSpec((rows, TOP_K), lambda ib, it, ps: (ib * nt + it, 0)),
                  pl.BlockSpec(memory_space=pl.ANY),
                  pl.BlockSpec((bb, tt, d), lambda ib, it, ps: (ib, it, 0)),
                  pl.BlockSpec((bb, 1, d), lambda ib, it, ps: (ib, 0, 0)),
                  pl.BlockSpec((1, 1, d), lambda ib, it, ps: (0, 0, 0))],
        out_specs=pl.BlockSpec((bb, tt, d), lambda ib, it, ps: (ib, it, 0)),
        scratch_shapes=[pltpu.VMEM((TOP_K, rows, d), F32), pltpu.SemaphoreType.DMA(())],
    )
    return pl.pallas_call(
        _combine_kernel,
        out_shape=jax.ShapeDtypeStruct((b, t, d), F32),
        grid_spec=grid_spec,
        compiler_params=_cparams("arbitrary", "arbitrary"),
        name="moe_combine",
    )(pad_start, eid, rank, wt_rows, y, x1, gt, g_post.reshape(1, 1, d))


def _hier_moe(h2, eid, wt, x1, gt_f, g_post, w13, w2):
    n = h2.shape[0]
    rank, cnt = _moe_rank(eid)
    counts = cnt[:, 0].astype(I32)
    padded = (counts + MOE_BLOCK - 1) // MOE_BLOCK * MOE_BLOCK
    pad_end = jnp.cumsum(padded)
    pad_start = (pad_end - padded).astype(I32)
    n_blocks = (n * TOP_K + N_EXPERTS * (MOE_BLOCK - 1) + MOE_BLOCK - 1) // MOE_BLOCK
    blk_eid = jnp.minimum(jnp.searchsorted(pad_end, jnp.arange(n_blocks, dtype=I32) * MOE_BLOCK, side="right"),
                          N_EXPERTS - 1).astype(I32)
    n_used = (pad_end[-1:] // MOE_BLOCK).astype(I32)
    xin = _moe_dispatch(pad_start, eid, rank, h2, n_blocks * MOE_BLOCK)
    y = _moe_experts(blk_eid, n_used, xin, w13, w2)
    return _moe_combine(pad_start, eid, rank, jnp.swapaxes(wt, 0, 1), y, x1, gt_f, g_post)


def _layer(x, mods, shift_prev, rkv_last, s0, cache, p, w):
    b, t, d = x.shape
    ca = p["w0"].shape[-1]
    sh_a, sc_a, gt_a, sh_f, sc_f, gt_f = mods
    r, k, v, q, kb, vb = _in_proj(x, sc_a, sh_a, p["g_pre_mix"], w["w_main"], [F32, F32, F32, BF16, F32, F32])
    wl, a, gate, logf, shift_new = _lora_heads(x, sc_a, sh_a, p["g_pre_mix"], shift_prev, p)
    ya, s_new = _rwkv7(r, k, v, wl, a, gate, rkv_last[..., :ca], rkv_last[..., ca:2 * ca], rkv_last[..., 2 * ca:],
                       s0, p)
    if cache is None:
        fq, fk = _forget_layouts(logf, None)
        yb = _fox_prompt(q, kb, vb, fq, fk)
    else:
        k_past, v_past, logf_past = cache
        past = k_past.shape[1]
        fq, fk = _forget_layouts(logf, logf_past)
        yb = _fox_sample(q, kb, vb, k_past.reshape(b, past, -1), v_past.reshape(b, past, -1), fq, fk)
    x1, h2, eid, wt = _postmix(ya, yb, x, gt_a, sc_f, sh_f, p["g_post_mix"], p["g_pre_ffn"],
                               w["wo_a"], w["wo_b"], w["wr_hi"], w["wr_lo"], w["br"])
    out = _hier_moe(h2, eid, wt, x1, gt_f, p["g_post_ffn"], w["w13"], w["w2"])
    nh = logf.shape[-1]
    return (out, shift_new.reshape(b, d), _pairs_to_state(s_new),
            kb.reshape(b, t, nh, HEAD_DIM), vb.reshape(b, t, nh, HEAD_DIM), logf)


def kernel(x_prompt, x_sample, state_shift, state_wkv, cache_k, cache_v, cache_logf, c_prompt, c_sample, w_ada, b_ada, g_pre_mix, g_post_mix, g_pre_ffn, g_post_ffn, w_in, b_f, mu_rkv, mu_wag, w0, w_lora_a, w_lora_b, a0, a_lora_a, a_lora_b, g_lora_a, g_lora_b, k_k, k_a, r_k, ln_x_w, ln_x_b, w_out, w_rg, b_rg, w_re, b_re, w1, w3, w2):
    stacked = dict(w_ada=w_ada, b_ada=b_ada, g_pre_mix=g_pre_mix, g_post_mix=g_post_mix, g_pre_ffn=g_pre_ffn,
                   g_post_ffn=g_post_ffn, w_in=w_in, b_f=b_f, mu_rkv=mu_rkv, mu_wag=mu_wag, w0=w0,
                   w_lora_a=w_lora_a, w_lora_b=w_lora_b, a0=a0, a_lora_a=a_lora_a, a_lora_b=a_lora_b,
                   g_lora_a=g_lora_a, g_lora_b=g_lora_b, k_k=k_k, k_a=k_a, r_k=r_k, ln_x_w=ln_x_w,
                   ln_x_b=ln_x_b, w_out=w_out, w_rg=w_rg, b_rg=b_rg, w_re=w_re, b_re=b_re, w1=w1, w3=w3, w2=w2)
    depth = w_ada.shape[0]
    assert depth == 1, "one layer per call"
    p = {name: arr[0] for name, arr in stacked.items()}
    bp, _, d = x_prompt.shape
    bs = x_sample.shape[0]
    ca = p["w0"].shape[-1]
    n_main = 3 * ca + 3 * (d - ca)
    p["w_forget"] = p["w_in"][:, n_main:]

    wr = _pad_to(jnp.concatenate([p["w_re"], p["w_rg"]], axis=1).T, 0, ROUTER_ROWS)
    wr_hi = wr.astype(BF16)
    w = dict(
        w_main=p["w_in"][:, :n_main].astype(BF16),
        wo_a=p["w_out"][:ca].astype(BF16), wo_b=p["w_out"][ca:].astype(BF16),
        wr_hi=wr_hi, wr_lo=(wr - wr_hi.astype(F32)).astype(BF16),
        br=_pad_to(jnp.concatenate([p["b_re"], p["b_rg"]]), 0, ROUTER_ROWS).reshape(ROUTER_ROWS, 1),
        w13=jnp.concatenate([p["w1"], p["w3"]], axis=-1).astype(BF16), w2=p["w2"].astype(BF16),
    )

    mod = _ada_mod(jnp.concatenate([c_prompt, c_sample], axis=0), p["w_ada"], p["b_ada"])
    mods = [mod[:, None, i * d:(i + 1) * d] for i in range(6)]
    mods_p = [m[:bp] for m in mods]
    mods_s = [m[bp:] for m in mods]

    n_pairs = ca // PAIR
    out_p = _layer(x_prompt, mods_p, jnp.zeros((bp, 1, d), F32), jnp.zeros((bp, 1, 3 * ca), F32),
                   jnp.zeros((bp, n_pairs, PAIR, PAIR), F32), None, p, w)
    rkv_last = _matmul_rows(state_shift[0], w["w_main"][:, :3 * ca], tn=ca)[:, None, :]
    out_s = _layer(x_sample, mods_s, state_shift[0][:, None, :], rkv_last, _state_to_pairs(state_wkv[0]),
                   (cache_k[0], cache_v[0], cache_logf[0]), p, w)
    yp, shp, wkvp, kp, vp, lfp = out_p
    ys, shs, wkvs, ks_, vs_, lfs = out_s
    return (yp, ys, shp[None], wkvp[None], kp[None], vp[None], lfp[None],
            shs[None], wkvs[None], ks_[None], vs_[None], lfs[None])
```

```python
import functools
import math

import jax
import jax.numpy as jnp
from jax import lax
from jax.experimental import pallas as pl
from jax.experimental.pallas import tpu as pltpu

F32 = jnp.float32
BF16 = jnp.bfloat16
I32 = jnp.int32

HEAD_DIM = 64
LANES = 128
PAIR = 2 * HEAD_DIM
RMS_EPS = 1e-6
GN_EPS = 64e-5
N_GROUPS = 4
EXPERTS_PER_GROUP = 8
N_EXPERTS = N_GROUPS * EXPERTS_PER_GROUP
TOP_K = 2
CHUNK = 64
VMEM_LIMIT_BYTES = 56 * 1024 * 1024
ROW_TILE = 512
MOE_BLOCK = 256


def _cparams(*sem):
    return pltpu.CompilerParams(dimension_semantics=sem, vmem_limit_bytes=VMEM_LIMIT_BYTES)


def _dot(a, b):
    return jnp.dot(a, b, preferred_element_type=F32)


def _dot_nt(a, b):
    return lax.dot_general(a, b, (((1,), (1,)), ((), ())), preferred_element_type=F32)


def _dot_tn(a, b):
    return lax.dot_general(a, b, (((0,), (0,)), ((), ())), preferred_element_type=F32)


def _split3(x):
    hi = x.astype(BF16)
    r1 = x - hi.astype(F32)
    mid = r1.astype(BF16)
    lo = (r1 - mid.astype(F32)).astype(BF16)
    return hi, mid, lo


def _dot3_exact_rhs(x, m):
    hi, mid, lo = _split3(x)
    return _dot(hi, m) + _dot(mid, m) + _dot(lo, m)


def _sigmoid(x):
    return 1.0 / (1.0 + jnp.exp(-x))


def _log_sigmoid(x):
    return jnp.minimum(x, 0.0) - jnp.log(1.0 + jnp.exp(-jnp.abs(x)))


def _mod_norm(x, g, scale, shift):
    y = x * lax.rsqrt(jnp.mean(x * x, axis=-1, keepdims=True) + RMS_EPS)
    return (y * g) * (1.0 + scale) + shift


def _row_tiling(batch, seq):
    if seq >= ROW_TILE:
        assert seq % ROW_TILE == 0
        return 1, ROW_TILE
    assert ROW_TILE % seq == 0 and batch % (ROW_TILE // seq) == 0
    return ROW_TILE // seq, seq


def _ada_kernel(c_ref, w_ref, b_ref, o_ref):
    c = c_ref[...]
    s = c * _sigmoid(c)
    o_ref[...] = _dot(s.astype(BF16), w_ref[...].astype(BF16)) + b_ref[...]


def _ada_mod(c_all, w_ada, b_ada):
    m, d = c_all.shape
    n = w_ada.shape[1]
    tn = 1024
    return pl.pallas_call(
        _ada_kernel,
        out_shape=jax.ShapeDtypeStruct((m, n), F32),
        grid=(n // tn,),
        in_specs=[pl.BlockSpec((m, d), lambda j: (0, 0)),
                  pl.BlockSpec((d, tn), lambda j: (0, j)),
                  pl.BlockSpec((1, tn), lambda j: (0, j))],
        out_specs=pl.BlockSpec((m, tn), lambda j: (0, j)),
        compiler_params=_cparams("arbitrary"),
        name="ada_mod",
    )(c_all, w_ada, b_ada.reshape(1, n))


def _proj_kernel(x_ref, sc_ref, sh_ref, g_ref, w_ref, *refs, out_scales):
    outs, hs_ref = refs[:-1], refs[-1]
    j = pl.program_id(2)
    bb, tt, d = x_ref.shape

    @pl.when(j == 0)
    def _():
        h = _mod_norm(x_ref[...], g_ref[...], sc_ref[...], sh_ref[...])
        hs_ref[...] = h.reshape(bb * tt, d).astype(BF16)

    acc = _dot(hs_ref[...], w_ref[...])
    for idx, o_ref in enumerate(outs):
        @pl.when(j == idx)
        def _(o_ref=o_ref, scale=out_scales[idx]):
            scaled = acc if scale == 1.0 else acc * scale
            o_ref[...] = scaled.reshape(o_ref.shape).astype(o_ref.dtype)


def _in_proj(x, sc, sh, g, w_main, out_dtypes, out_scales):
    b, t, d = x.shape
    n_out = len(out_dtypes)
    cw = w_main.shape[1] // n_out
    bb, tt = _row_tiling(b, t)
    row_spec = pl.BlockSpec((bb, tt, d), lambda ib, it, j: (ib, it, 0))
    mod_spec = pl.BlockSpec((bb, 1, d), lambda ib, it, j: (ib, 0, 0))
    out_spec = pl.BlockSpec((bb, tt, cw), lambda ib, it, j: (ib, it, 0))
    return pl.pallas_call(
        functools.partial(_proj_kernel, out_scales=tuple(out_scales)),
        out_shape=[jax.ShapeDtypeStruct((b, t, cw), dt) for dt in out_dtypes],
        grid=(b // bb, t // tt, n_out),
        in_specs=[row_spec, mod_spec, mod_spec,
                  pl.BlockSpec((1, 1, d), lambda ib, it, j: (0, 0, 0)),
                  pl.BlockSpec((d, cw), lambda ib, it, j: (0, j))],
        out_specs=[out_spec] * n_out,
        scratch_shapes=[pltpu.VMEM((bb * tt, d), BF16)],
        compiler_params=_cparams("arbitrary", "arbitrary", "arbitrary"),
        name="in_proj",
    )(x, sc, sh, g.reshape(1, 1, d), w_main)


def _lora_kernel(x_ref, sc_ref, sh_ref, g_ref, prev_ref, muw_ref, mua_ref, mug_ref,
                 aw_ref, bw_ref, w0_ref, aa_ref, ba_ref, a0_ref, ag_ref, bg_ref, wf_ref, bf_ref,
                 wl_ref, a_ref, gate_ref, lf_ref, shn_ref, carry_ref):
    it = pl.program_id(1)
    bb, tt, d = x_ref.shape
    rows = bb * tt
    h = _mod_norm(x_ref[...], g_ref[...], sc_ref[...], sh_ref[...])

    @pl.when(it == 0)
    def _():
        carry_ref[...] = prev_ref[...]

    rolled = pltpu.roll(h.reshape(rows, d), 1, 0).reshape(bb, tt, d)
    tpos = lax.broadcasted_iota(I32, (bb, tt, d), 1)
    h_prev = jnp.where(tpos == 0, carry_ref[...], rolled)
    last = h[:, tt - 1:tt, :]
    carry_ref[...] = last
    shn_ref[...] = last

    xx = h_prev - h

    def mix(mu_ref):
        return (h + xx * mu_ref[...]).reshape(rows, d).astype(BF16)

    ca = wl_ref.shape[-1]
    lw = jnp.tanh(_dot(mix(muw_ref), aw_ref[...]))
    w = w0_ref[...] + _dot(lw.astype(BF16), bw_ref[...])
    wl_ref[...] = (-math.exp(-0.5) * _sigmoid(w)).reshape(bb, tt, ca)
    la = _dot(mix(mua_ref), aa_ref[...])
    a_ref[...] = _sigmoid(a0_ref[...] + _dot(la.astype(BF16), ba_ref[...])).reshape(bb, tt, ca)
    lg = _sigmoid(_dot(mix(mug_ref), ag_ref[...]))
    gate_ref[...] = _dot(lg.astype(BF16), bg_ref[...]).reshape(bb, tt, ca).astype(gate_ref.dtype)
    nh = lf_ref.shape[-1]
    z = _dot(h.reshape(rows, d).astype(BF16), wf_ref[...]) + bf_ref[...]
    lf_ref[...] = _log_sigmoid(z)[:, :nh].reshape(bb, tt, nh)


def _pad_to(x, axis, size):
    pad = [(0, 0)] * x.ndim
    pad[axis] = (0, size - x.shape[axis])
    return jnp.pad(x, pad)


def _round_up(n, m):
    return (n + m - 1) // m * m


def _lora_heads(x, sc, sh, g, shift_prev, p):
    b, t, d = x.shape
    ca = p["w0"].shape[-1]
    nh = p["b_f"].shape[-1]
    bb, tt = _row_tiling(b, t)

    def lora_pair(a_w, b_w):
        r = _round_up(a_w.shape[1], LANES)
        return _pad_to(a_w, 1, r).astype(BF16), _pad_to(b_w, 0, r).astype(BF16)

    aw, bw = lora_pair(p["w_lora_a"], p["w_lora_b"])
    aa, ba = lora_pair(p["a_lora_a"], p["a_lora_b"])
    ag, bg = lora_pair(p["g_lora_a"], p["g_lora_b"])
    wf = _pad_to(p["w_forget"], 1, LANES).astype(BF16)
    bf = _pad_to(p["b_f"].reshape(1, nh), 1, LANES)
    mu = p["mu_wag"].reshape(3, 1, 1, d)

    row_spec = pl.BlockSpec((bb, tt, d), lambda ib, it: (ib, it, 0))
    mod_spec = pl.BlockSpec((bb, 1, d), lambda ib, it: (ib, 0, 0))
    vec_spec = pl.BlockSpec((1, 1, d), lambda ib, it: (0, 0, 0))

    def full(a):
        return pl.BlockSpec(a.shape, lambda ib, it: (0,) * a.ndim)

    ca_spec = pl.BlockSpec((bb, tt, ca), lambda ib, it: (ib, it, 0))
    weights = [aw, bw, p["w0"].reshape(1, ca), aa, ba, p["a0"].reshape(1, ca), ag, bg, wf, bf]
    return pl.pallas_call(
        _lora_kernel,
        out_shape=[jax.ShapeDtypeStruct((b, t, ca), F32), jax.ShapeDtypeStruct((b, t, ca), F32),
                   jax.ShapeDtypeStruct((b, t, ca), BF16), jax.ShapeDtypeStruct((b, t, nh), F32),
                   jax.ShapeDtypeStruct((b, 1, d), F32)],
        grid=(b // bb, t // tt),
        in_specs=[row_spec, mod_spec, mod_spec, vec_spec, mod_spec, vec_spec, vec_spec, vec_spec]
                 + [full(a) for a in weights],
        out_specs=[ca_spec, ca_spec, ca_spec,
                   pl.BlockSpec((bb, tt, nh), lambda ib, it: (ib, it, 0)),
                   pl.BlockSpec((bb, 1, d), lambda ib, it: (ib, 0, 0))],
        scratch_shapes=[pltpu.VMEM((bb, 1, d), F32)],
        compiler_params=_cparams("arbitrary", "arbitrary"),
        name="lora_heads",
    )(x, sc, sh, g.reshape(1, 1, d), shift_prev, mu[0], mu[1], mu[2], *weights)


def _cumsum_kernel(x_ref, o_ref, carry_ref):
    it = pl.program_id(1)
    bb, nh, tl = x_ref.shape

    @pl.when(it == 0)
    def _():
        carry_ref[...] = jnp.zeros_like(carry_ref)

    r = lax.broadcasted_iota(I32, (tl, tl), 0)
    c = lax.broadcasted_iota(I32, (tl, tl), 1)
    upper = jnp.where(r <= c, 1.0, 0.0).astype(BF16)
    cs = _dot3_exact_rhs(x_ref[...].reshape(bb * nh, tl), upper) + carry_ref[...]
    o_ref[...] = cs.reshape(bb, nh, tl)
    carry_ref[...] = cs[:, tl - 1:tl]


def _cumsum_lanes(x):
    b, nh, l = x.shape
    tl = l if l <= 2304 else 1024
    assert l % tl == 0 and nh % 8 == 0
    bb = math.gcd(b, 8)
    return pl.pallas_call(
        _cumsum_kernel,
        out_shape=jax.ShapeDtypeStruct((b, nh, l), F32),
        grid=(b // bb, l // tl),
        in_specs=[pl.BlockSpec((bb, nh, tl), lambda ib, it: (ib, 0, it))],
        out_specs=pl.BlockSpec((bb, nh, tl), lambda ib, it: (ib, 0, it)),
        scratch_shapes=[pltpu.VMEM((bb * nh, 1), F32)],
        compiler_params=_cparams("arbitrary", "arbitrary"),
        name="logf_cumsum",
    )(x)


NEG_BIG = -1e30


def _head_masked(q):
    lane = lax.broadcasted_iota(I32, q.shape, 1)
    first = lane < HEAD_DIM
    zero = jnp.zeros_like(q)
    return first, (jnp.where(first, q, zero), jnp.where(first, zero, q))


FOX_TILE = 512
LOG2E = 1.0 / math.log(2.0)
Q_PRESCALE = HEAD_DIM ** -0.5 * LOG2E


def _with_bias_lanes(x, first, f_col, own_sign):
    lane = lax.broadcasted_iota(I32, x.shape, 1)
    out = []
    for e in range(2):
        keep = first if e == 0 else jnp.logical_not(first)
        base = HEAD_DIM if e == 0 else 0
        terms = [t.astype(F32) * own_sign for t in _split3(f_col[:, e:e + 1])]
        ones = jnp.ones_like(terms[0])
        cols = terms + [ones] * 3 if own_sign < 0 else [ones] * 3 + terms
        y = jnp.where(keep, x, 0.0)
        for i, col in enumerate(cols):
            y = jnp.where(lane == base + i, col, y)
        out.append(y.astype(BF16))
    return out


def _fox_prompt_kernel(q_ref, k_ref, v_ref, f_ref, o_ref, k0_ref, k1_ref, vt_ref):
    t = k_ref.shape[0]
    tq = tk = FOX_TILE
    half = HEAD_DIM
    first_t = lax.broadcasted_iota(I32, (t, PAIR), 1) < half
    kaug = _with_bias_lanes(k_ref[...], first_t, f_ref[...], -1.0)
    k0_ref[...] = kaug[0]
    k1_ref[...] = kaug[1]
    kaug_refs = (k0_ref, k1_ref)
    for i in range(t // tk):
        vt_ref[:, i * tk:(i + 1) * tk] = v_ref[i * tk:(i + 1) * tk, :].T.astype(BF16)

    first_q = lax.broadcasted_iota(I32, (tq, PAIR), 1) < half
    krow = lax.broadcasted_iota(I32, (tk, tq), 0)
    qcol = lax.broadcasted_iota(I32, (tk, tq), 1)

    def q_block(qi, carry):
        q0 = pl.multiple_of(qi * tq, tq)
        qaug = _with_bias_lanes(q_ref[pl.ds(q0, tq), :].astype(F32), first_q, f_ref[pl.ds(q0, tq), :], 1.0)

        def kv_blocks(state, blocks):
            starts = [pl.multiple_of(k0, tk) for k0, _ in blocks]
            s = [[_dot_nt(kaug_refs[e][pl.ds(k0, tk), :], qaug[e]) for e in range(2)] for k0 in starts]
            state = list(state)
            for (_, masked), k0, s_blk in zip(blocks, starts, s):
                for e in range(2):
                    m_prev, l_prev, acc = state[e]
                    s_e = jnp.where(krow <= qcol, s_blk[e], NEG_BIG) if masked else s_blk[e]
                    m_new = jnp.maximum(m_prev, jnp.max(s_e, axis=0, keepdims=True))
                    p = jnp.exp2(s_e - m_new)
                    alpha = jnp.exp2(m_prev - m_new)
                    l_new = alpha * l_prev + jnp.sum(p, axis=0, keepdims=True)
                    pv = _dot(vt_ref[e * half:(e + 1) * half, pl.ds(k0, tk)], p.astype(BF16))
                    state[e] = (m_new, l_new, acc * alpha + pv)
            return tuple(state)

        neg = jnp.full((1, tq), NEG_BIG, F32)
        zero = jnp.zeros((1, tq), F32)
        init = tuple((neg, zero, jnp.zeros((half, tq), F32)) for _ in range(2))
        state = lax.fori_loop(
            0, qi // 2, lambda i, st: kv_blocks(st, [(2 * i * tk, False), ((2 * i + 1) * tk, False)]), init)
        state = lax.cond(
            qi % 2 == 1,
            lambda st: kv_blocks(st, [((qi - 1) * tk, False), (qi * tk, True)]),
            lambda st: kv_blocks(st, [(qi * tk, True)]),
            state)
        o_t = jnp.concatenate([state[e][2] / state[e][1] for e in range(2)], axis=0)
        o_ref[pl.ds(q0, tq), :] = o_t.T.astype(o_ref.dtype)
        return carry

    lax.fori_loop(0, t // tq, q_block, 0)


def _fox_prompt(q, k, v, f_rows):
    b, t, cb = q.shape
    hp = cb // PAIR
    assert t % FOX_TILE == 0
    seq_spec = pl.BlockSpec((None, t, PAIR), lambda ib, ih: (ib, 0, ih))
    return pl.pallas_call(
        _fox_prompt_kernel,
        out_shape=jax.ShapeDtypeStruct((b, t, cb), BF16),
        grid=(b, hp),
        in_specs=[seq_spec, seq_spec, seq_spec,
                  pl.BlockSpec((None, None, t, 2), lambda ib, ih: (ib, ih, 0, 0))],
        out_specs=seq_spec,
        scratch_shapes=[pltpu.VMEM((t, PAIR), BF16), pltpu.VMEM((t, PAIR), BF16), pltpu.VMEM((PAIR, t), BF16)],
        compiler_params=_cparams("arbitrary", "arbitrary"),
        name="fox_prompt",
    )(q, k, v, f_rows)


def _fox_sample_kernel(q_ref, kn_ref, vn_ref, kc_ref, vc_ref, fq_ref, fk_ref, o_ref):
    ts = q_ref.shape[0]
    past = kc_ref.shape[0]
    n_pairs = q_ref.shape[1] // PAIR
    row = lax.broadcasted_iota(I32, (ts, ts), 0)
    col = lax.broadcasted_iota(I32, (ts, ts), 1)
    for pp in range(n_pairs):
        lanes = slice(pp * PAIR, (pp + 1) * PAIR)
        first, qs = _head_masked(q_ref[:, lanes])
        kc = kc_ref[:, lanes]
        vc = vc_ref[:, lanes]
        kn = kn_ref[:, lanes].astype(BF16)
        vn = vn_ref[:, lanes].astype(BF16)
        outs = []
        for e in range(2):
            fq = fq_ref[pp, :, e:e + 1]
            s_p = _dot_nt(qs[e], kc) + (fq - fk_ref[pp, e:e + 1, 0:past])
            s_n = _dot_nt(qs[e], kn) + (fq - fk_ref[pp, e:e + 1, past:past + ts])
            s_n = jnp.where(col <= row, s_n, NEG_BIG)
            m = jnp.maximum(jnp.max(s_p, axis=1, keepdims=True), jnp.max(s_n, axis=1, keepdims=True))
            p_p = jnp.exp2(s_p - m)
            p_n = jnp.exp2(s_n - m)
            l = jnp.sum(p_p, axis=1, keepdims=True) + jnp.sum(p_n, axis=1, keepdims=True)
            outs.append((_dot(p_p.astype(BF16), vc) + _dot(p_n.astype(BF16), vn)) / l)
        o_ref[:, lanes] = jnp.where(first, outs[0], outs[1]).astype(o_ref.dtype)


def _fox_sample(q, kn, vn, kc, vc, fq, fk):
    b, ts, cb = q.shape
    past = kc.shape[1]
    lp = fk.shape[-1]
    lb = min(cb, 4 * PAIR)
    npb = lb // PAIR
    new_spec = pl.BlockSpec((None, ts, lb), lambda ib, ig: (ib, 0, ig))
    past_spec = pl.BlockSpec((None, past, lb), lambda ib, ig: (ib, 0, ig))
    return pl.pallas_call(
        _fox_sample_kernel,
        out_shape=jax.ShapeDtypeStruct((b, ts, cb), BF16),
        grid=(b, cb // lb),
        in_specs=[new_spec, new_spec, new_spec, past_spec, past_spec,
                  pl.BlockSpec((None, npb, ts, 2), lambda ib, ig: (ib, ig, 0, 0)),
                  pl.BlockSpec((None, npb, 2, lp), lambda ib, ig: (ib, ig, 0, 0))],
        out_specs=new_spec,
        compiler_params=_cparams("arbitrary", "arbitrary"),
        name="fox_sample",
    )(q, kn, vn, kc, vc, fq, fk)


def _forget_layouts(logf_new, logf_past):
    b, t, nh = logf_new.shape
    lt = jnp.swapaxes(logf_new, 1, 2)
    past = 0
    if logf_past is not None:
        past = logf_past.shape[1]
        lt = jnp.concatenate([jnp.swapaxes(logf_past, 1, 2), lt], axis=2)
    lp = _round_up(past + t, LANES)
    ft = _cumsum_lanes(_pad_to(lt, 2, lp)) * LOG2E
    fk = ft.reshape(b, nh // 2, 2, lp)
    fq = jnp.swapaxes(fk[..., past:past + t], 2, 3)
    return fq, fk


def _split2(x):
    hi = x.astype(BF16)
    return hi, (x - hi.astype(F32)).astype(BF16)


def _rwkv_kernel(r_ref, k_ref, v_ref, wl_ref, a_ref, g_ref, rl_ref, kl_ref, vl_ref, s0_ref,
                 mur_ref, muk_ref, muv_ref, kk_ref, ka_ref, rk_ref, lnw_ref, lnb_ref,
                 y_ref, sout_ref, st_ref, prev_ref):
    itb = pl.program_id(2)
    tb, lanes = r_ref.shape
    n_pairs = lanes // PAIR
    c = min(CHUNK, tb)

    @pl.when(itb == 0)
    def _():
        st_ref[...] = s0_ref[...]
        prev_ref[0:1, :] = rl_ref[...]
        prev_ref[1:2, :] = kl_ref[...]
        prev_ref[2:3, :] = vl_ref[...]

    pairs = range(n_pairs)
    heads = range(2)
    ri = lax.broadcasted_iota(I32, (c, 2 * c), 0)
    ci = lax.broadcasted_iota(I32, (c, 2 * c), 1) % c
    strict2 = ri > ci
    incl2 = ri >= ci
    tri_ones = jnp.where(incl2[:, :c], 1.0, 0.0).astype(BF16)
    li = lax.broadcasted_iota(I32, (PAIR, PAIR), 0) // HEAD_DIM
    lj = lax.broadcasted_iota(I32, (PAIR, PAIR), 1) // HEAD_DIM
    same_head = li == lj
    head_ones = jnp.where(same_head, 1.0, 0.0).astype(BF16)
    first = lax.broadcasted_iota(I32, (c, PAIR), 1) < HEAD_DIM
    row0 = lax.broadcasted_iota(I32, (c, PAIR), 0) == 0
    zeros_cb = jnp.zeros((c, PAIR), BF16)

    def head_sum2(x):
        hi, lo = _split2(x)
        s = _dot(jnp.concatenate([hi, lo], axis=0), head_ones)
        return s[:c] + s[c:]

    def shifted(x, prev_row):
        return jnp.where(row0, prev_row, pltpu.roll(x, 1, 0))

    def by_head(x):
        return [jnp.where(first, x, zeros_cb), jnp.where(first, zeros_cb, x)]

    def chunk(ic, carry):
        rows = pl.ds(pl.multiple_of(ic * c, c), c)
        lns = [slice(pp * PAIR, (pp + 1) * PAIR) for pp in pairs]
        r, k, v, wl, a, kk = [], [], [], [], [], []
        for ln in lns:
            r_raw, k_raw, v_raw = r_ref[rows, ln], k_ref[rows, ln], v_ref[rows, ln]
            r.append(r_raw + mur_ref[:, ln] * (shifted(r_raw, prev_ref[0:1, ln]) - r_raw))
            k.append(k_raw + muk_ref[:, ln] * (shifted(k_raw, prev_ref[1:2, ln]) - k_raw))
            v.append(v_raw + muv_ref[:, ln] * (shifted(v_raw, prev_ref[2:3, ln]) - v_raw))
            prev_ref[0:1, ln] = r_raw[c - 1:c, :]
            prev_ref[1:2, ln] = k_raw[c - 1:c, :]
            prev_ref[2:3, ln] = v_raw[c - 1:c, :]
            wl.append(wl_ref[rows, ln])
            a.append(a_ref[rows, ln])
            kk.append(k[-1] * kk_ref[:, ln])
        lcum = []
        for p in pairs:
            s = _dot(tri_ones, jnp.concatenate(_split3(wl[p]), axis=1))
            lcum.append(s[:, :PAIR] + s[:, PAIR:2 * PAIR] + s[:, 2 * PAIR:])
        kk_ss = [head_sum2(x * x) for x in kk]
        kf = [k[p] * (1.0 + (a[p] - 1.0) * ka_ref[:, lns[p]]) for p in pairs]
        bonus_s = [_dot((r[p] * kf[p] * rk_ref[:, lns[p]]).astype(BF16), head_ones) for p in pairs]
        kk = [kk[p] / jnp.maximum(jnp.sqrt(kk_ss[p]), 1e-12) for p in pairs]
        b = [kk[p] * a[p] for p in pairs]
        inv = [jnp.exp(-lcum[p]) for p in pairs]
        ltot = [lcum[p][c - 1:c, :] for p in pairs]
        suffix = [jnp.exp(ltot[p] - lcum[p]) for p in pairs]
        r_bar = [(r[p] * jnp.exp(lcum[p])).astype(BF16) for p in pairs]
        a_bar = [(-kk[p] * jnp.exp(lcum[p] - wl[p])).astype(BF16) for p in pairs]
        bk_bar = [jnp.concatenate([(b[p] * inv[p]).astype(BF16), (kf[p] * inv[p]).astype(BF16)], axis=0)
                  for p in pairs]
        bk_suf = [jnp.concatenate([(b[p] * suffix[p]).astype(BF16), (kf[p] * suffix[p]).astype(BF16)], axis=0)
                  for p in pairs]
        vb = [x.astype(BF16) for x in v]
        st = [st_ref[p] for p in pairs]
        stb = [x.astype(BF16) for x in st]

        prod = [_dot_nt(jnp.concatenate(by_head(a_bar[p]) + by_head(r_bar[p]), axis=0), bk_bar[p]) for p in pairs]
        a_full = [[jnp.where(strict2, prod[p][e * c:(e + 1) * c], 0.0).astype(BF16) for e in heads] for p in pairs]
        r_full = [[jnp.where(incl2, prod[p][(2 + e) * c:(3 + e) * c], 0.0).astype(BF16) for e in heads]
                  for p in pairs]
        on_s = [_dot_nt(jnp.concatenate([a_bar[p], r_bar[p]], axis=0), stb[p]) for p in pairs]
        zv = [jnp.concatenate([zeros_cb, vb[p]], axis=0) for p in pairs]
        rhs_v = [[_dot(a_full[p][e], zv[p]) for e in heads] for p in pairs]
        x = [[on_s[p][:c] + jnp.where(first, rhs_v[p][0], rhs_v[p][1])] * 2 for p in pairs]
        x = [list(xp) for xp in x]
        ap = [[a_full[p][e][:, :c] for e in heads] for p in pairs]
        span = 1
        while True:
            span *= 2
            if span >= c:
                x = [[x[p][e] + _dot(ap[p][e], x[p][e].astype(BF16)) for e in heads] for p in pairs]
                break
            res = [[_dot(ap[p][e], jnp.concatenate([x[p][e].astype(BF16), ap[p][e]], axis=1)) for e in heads]
                   for p in pairs]
            x = [[x[p][e] + res[p][e][:, :PAIR] for e in heads] for p in pairs]
            ap = [[res[p][e][:, PAIR:].astype(BF16) for e in heads] for p in pairs]
        uv = [jnp.concatenate([jnp.where(first, x[p][0], x[p][1]).astype(BF16), vb[p]], axis=0) for p in pairs]

        y_h = [[_dot(r_full[p][e], uv[p]) for e in heads] for p in pairs]
        upd = [_dot_tn(uv[p], bk_suf[p]) for p in pairs]
        for p in pairs:
            st_ref[p] = st[p] * jnp.exp(ltot[p]) + jnp.where(same_head, upd[p], 0.0)

        y = [on_s[p][c:] + jnp.where(first, y_h[p][0], y_h[p][1]) for p in pairs]
        mean = [head_sum2(y[p]) * (1.0 / HEAD_DIM) for p in pairs]
        dy = [y[p] - mean[p] for p in pairs]
        var = [head_sum2(dy[p] * dy[p]) * (1.0 / HEAD_DIM) for p in pairs]
        for p in pairs:
            ln = lns[p]
            yn = dy[p] * lax.rsqrt(var[p] + GN_EPS) * lnw_ref[:, ln] + lnb_ref[:, ln]
            y_ref[rows, ln] = ((yn + bonus_s[p] * v[p]) * g_ref[rows, ln].astype(F32)).astype(y_ref.dtype)
        return carry

    lax.fori_loop(0, tb // c, chunk, 0)
    sout_ref[...] = st_ref[...]


def _rwkv7(r, k, v, wl, a, g, r_last, k_last, v_last, s0, p, pairs_per_step=8):
    b, t, ca = r.shape
    n_pairs = ca // PAIR
    pps = min(pairs_per_step, n_pairs)
    lanes = pps * PAIR
    tb = min(t, 256)
    assert t % tb == 0 and tb % min(CHUNK, tb) == 0
    seq_spec = pl.BlockSpec((None, tb, lanes), lambda ib, ig, it: (ib, it, ig))
    row_spec = pl.BlockSpec((None, 1, lanes), lambda ib, ig, it: (ib, 0, ig))
    st_spec = pl.BlockSpec((None, pps, PAIR, PAIR), lambda ib, ig, it: (ib, ig, 0, 0))
    par_spec = pl.BlockSpec((1, lanes), lambda ib, ig, it: (0, ig))
    mu = p["mu_rkv"].reshape(3, 1, ca)
    vecs = [mu[0], mu[1], mu[2], p["k_k"].reshape(1, ca), p["k_a"].reshape(1, ca), p["r_k"].reshape(1, ca),
            p["ln_x_w"].reshape(1, ca), p["ln_x_b"].reshape(1, ca)]
    return pl.pallas_call(
        _rwkv_kernel,
        out_shape=[jax.ShapeDtypeStruct((b, t, ca), BF16), jax.ShapeDtypeStruct(s0.shape, F32)],
        grid=(b, n_pairs // pps, t // tb),
        in_specs=[seq_spec] * 6 + [row_spec] * 3 + [st_spec] + [par_spec] * 8,
        out_specs=[seq_spec, st_spec],
        scratch_shapes=[pltpu.VMEM((pps, PAIR, PAIR), F32), pltpu.VMEM((8, lanes), F32)],
        compiler_params=_cparams("arbitrary", "arbitrary", "arbitrary"),
        name="rwkv7",
    )(r, k, v, wl, a, g, r_last, k_last, v_last, s0, *vecs)


def _state_to_pairs(s):
    b, h, n, _ = s.shape
    eye = jnp.eye(2, dtype=s.dtype)
    return jnp.einsum("bpejk,ef->bpejfk", s.reshape(b, h // 2, 2, n, n), eye).reshape(b, h // 2, 2 * n, 2 * n)


def _pairs_to_state(sp, n=HEAD_DIM):
    b, hp = sp.shape[:2]
    s6 = sp.reshape(b, hp, 2, n, 2, n)
    return jnp.stack([s6[:, :, 0, :, 0, :], s6[:, :, 1, :, 1, :]], axis=2).reshape(b, hp * 2, n, n)


def _matmul_kernel(a_ref, w_ref, o_ref):
    o_ref[...] = _dot(a_ref[...].astype(BF16), w_ref[...])


def _matmul_rows(a, w_bf16, tn=1024):
    m, kd = a.shape
    n = w_bf16.shape[1]
    assert n % tn == 0
    return pl.pallas_call(
        _matmul_kernel,
        out_shape=jax.ShapeDtypeStruct((m, n), F32),
        grid=(n // tn,),
        in_specs=[pl.BlockSpec((m, kd), lambda j: (0, 0)), pl.BlockSpec((kd, tn), lambda j: (0, j))],
        out_specs=pl.BlockSpec((m, tn), lambda j: (0, j)),
        compiler_params=_cparams("arbitrary"),
        name="matmul_rows",
    )(a, w_bf16)


ROUTER_ROWS = 40


def _rms(x, g):
    return x * lax.rsqrt(jnp.mean(x * x, axis=-1, keepdims=True) + RMS_EPS) * g


U32 = jnp.uint32
HIGH_HALF = 0xFFFF0000


def _pack_halves(x):
    n = x.shape[1] // 2
    lo = lax.bitcast_convert_type(x[:, :n].astype(BF16).astype(F32), U32)
    hi = lax.bitcast_convert_type(x[:, n:].astype(BF16).astype(F32), U32)
    return (hi & U32(HIGH_HALF)) | (lo >> U32(16))


def _unpack_halves(w):
    lo = lax.bitcast_convert_type(w << U32(16), F32)
    hi = lax.bitcast_convert_type(w & U32(HIGH_HALF), F32)
    return lo, hi


def _postmix_kernel(ya_ref, yb_ref, x_ref, gt_ref, sc_ref, sh_ref, gpost_ref, gpre_ref,
                    woa_ref, wob_ref, wrh_ref, wrl_ref, br_ref,
                    x1_ref, h2_ref, eid_ref, wt_ref):
    bb, tt, d = x_ref.shape
    rows = bb * tt
    o = (_dot(ya_ref[...].reshape(rows, ya_ref.shape[-1]), woa_ref[...])
         + _dot(yb_ref[...].reshape(rows, yb_ref.shape[-1]), wob_ref[...])).reshape(bb, tt, d)
    x1 = x_ref[...] + gt_ref[...] * _rms(o, gpost_ref[...])
    x1_ref[...] = x1
    h2 = _mod_norm(x1, gpre_ref[...], sc_ref[...], sh_ref[...]).reshape(rows, d)
    h2_ref[...] = _pack_halves(h2)

    hi = h2.astype(BF16)
    lo = (h2 - hi.astype(F32)).astype(BF16)
    logits = (_dot_nt(wrh_ref[...], hi) + _dot_nt(wrh_ref[...], lo) + _dot_nt(wrl_ref[...], hi)) + br_ref[...]
    le = logits[0:N_EXPERTS, :]
    lg = logits[N_EXPERTS:N_EXPERTS + N_GROUPS, :]
    gio = lax.broadcasted_iota(I32, lg.shape, 0)
    gmax = jnp.max(lg, axis=0, keepdims=True)
    gi = jnp.min(jnp.where(lg == gmax, gio, N_GROUPS), axis=0, keepdims=True)
    pg = 1.0 / jnp.sum(jnp.exp(lg - gmax), axis=0, keepdims=True)
    eio = lax.broadcasted_iota(I32, le.shape, 0)
    le1 = jnp.where(eio // EXPERTS_PER_GROUP == gi, le, NEG_BIG)
    m1 = jnp.max(le1, axis=0, keepdims=True)
    i1 = jnp.min(jnp.where(le1 == m1, eio, N_EXPERTS), axis=0, keepdims=True)
    le2 = jnp.where(eio == i1, NEG_BIG, le1)
    m2 = jnp.max(le2, axis=0, keepdims=True)
    i2 = jnp.min(jnp.where(le2 == m2, eio, N_EXPERTS), axis=0, keepdims=True)
    e2 = jnp.exp(m2 - m1)
    den = 1.0 + e2
    eid_ref[0:1, :] = i1
    eid_ref[1:2, :] = i2
    wt_ref[0:1, :] = pg / den
    wt_ref[1:2, :] = pg * e2 / den


def _postmix(ya, yb, x, gt, sc, sh, g_post, g_pre, wo_a, wo_b, wr_hi, wr_lo, br):
    b, t, d = x.shape
    bb, tt = _row_tiling(b, t)
    rows = bb * tt
    nt = t // tt
    n = b * t

    def row_spec(w):
        return pl.BlockSpec((bb, tt, w), lambda ib, it: (ib, it, 0))

    mod_spec = pl.BlockSpec((bb, 1, d), lambda ib, it: (ib, 0, 0))
    vec_spec = pl.BlockSpec((1, 1, d), lambda ib, it: (0, 0, 0))

    def full(a):
        return pl.BlockSpec(a.shape, lambda ib, it: (0,) * a.ndim)

    tok_spec = pl.BlockSpec((TOP_K, rows), lambda ib, it: (0, ib * nt + it))
    weights = [wo_a, wo_b, wr_hi, wr_lo, br]
    return pl.pallas_call(
        _postmix_kernel,
        out_shape=[jax.ShapeDtypeStruct((b, t, d), F32), jax.ShapeDtypeStruct((n, d // 2), U32),
                   jax.ShapeDtypeStruct((TOP_K, n), I32), jax.ShapeDtypeStruct((TOP_K, n), F32)],
        grid=(b // bb, nt),
        in_specs=[row_spec(ya.shape[-1]), row_spec(yb.shape[-1]), row_spec(d), mod_spec, mod_spec, mod_spec,
                  vec_spec, vec_spec] + [full(a) for a in weights],
        out_specs=[row_spec(d), pl.BlockSpec((rows, d // 2), lambda ib, it: (ib * nt + it, 0)), tok_spec, tok_spec],
        compiler_params=_cparams("arbitrary", "arbitrary"),
        name="postmix",
    )(ya, yb, x, gt, sc, sh, g_post.reshape(1, 1, d), g_pre.reshape(1, 1, d), *weights)


def _rank_kernel(eid_ref, rank_ref, cnt_ref, carry_ref):
    i = pl.program_id(0)
    tr = eid_ref.shape[1]

    @pl.when(i == 0)
    def _():
        carry_ref[...] = jnp.zeros_like(carry_ref)

    r = lax.broadcasted_iota(I32, (tr, tr), 0)
    c = lax.broadcasted_iota(I32, (tr, tr), 1)
    upper = jnp.where(r <= c, 1.0, 0.0).astype(BF16)
    eio = lax.broadcasted_iota(I32, (N_EXPERTS, tr), 0)
    base = carry_ref[...]
    for k in range(TOP_K):
        onehot = jnp.where(eio == eid_ref[k:k + 1, :], 1.0, 0.0)
        cum = _dot(onehot.astype(BF16), upper)
        rank = jnp.sum(onehot * (base + cum - onehot), axis=0, keepdims=True)
        rank_ref[k:k + 1, :] = rank.astype(I32)
        base = base + cum[:, tr - 1:tr]
    carry_ref[...] = base
    cnt_ref[...] = jnp.broadcast_to(base, cnt_ref.shape)


def _moe_rank(eid):
    n = eid.shape[1]
    tr = min(n, 512)
    assert n % tr == 0
    return pl.pallas_call(
        _rank_kernel,
        out_shape=[jax.ShapeDtypeStruct((TOP_K, n), I32), jax.ShapeDtypeStruct((N_EXPERTS, LANES), F32)],
        grid=(n // tr,),
        in_specs=[pl.BlockSpec((TOP_K, tr), lambda i: (0, i))],
        out_specs=[pl.BlockSpec((TOP_K, tr), lambda i: (0, i)),
                   pl.BlockSpec((N_EXPERTS, LANES), lambda i: (0, 0))],
        scratch_shapes=[pltpu.VMEM((N_EXPERTS, 1), F32)],
        compiler_params=_cparams("arbitrary"),
        name="moe_rank",
    )(eid)


def _dispatch_kernel(ps_ref, eid_ref, rank_ref, h_ref, xin0_ref, xin_ref, sem):
    del xin0_ref
    tt = h_ref.shape[0]

    def row_copy(t, dest):
        return pltpu.make_async_copy(h_ref.at[pl.ds(t, 1)], xin_ref.at[pl.ds(dest, 1)], sem)

    def issue(t, carry):
        for k in range(TOP_K):
            row_copy(t, ps_ref[eid_ref[k, t]] + rank_ref[k, t]).start()
        return carry

    def drain(t, carry):
        for k in range(TOP_K):
            row_copy(0, 0).wait()
        return carry

    lax.fori_loop(0, tt, issue, 0, unroll=8)
    lax.fori_loop(0, tt, drain, 0, unroll=8)


def _moe_dispatch(pad_start, eid, rank, h2, n_rows):
    n, d = h2.shape
    tt = min(n, MOE_BLOCK)
    smem_spec = pl.BlockSpec((TOP_K, tt), lambda i, ps: (0, i), memory_space=pltpu.SMEM)
    grid_spec = pltpu.PrefetchScalarGridSpec(
        num_scalar_prefetch=1,
        grid=(n // tt,),
        in_specs=[smem_spec, smem_spec,
                  pl.BlockSpec((tt, d), lambda i, ps: (i, 0)),
                  pl.BlockSpec(memory_space=pl.ANY)],
        out_specs=pl.BlockSpec(memory_space=pl.ANY),
        scratch_shapes=[pltpu.SemaphoreType.DMA(())],
    )
    return pl.pallas_call(
        _dispatch_kernel,
        out_shape=jax.ShapeDtypeStruct((n_rows, d), h2.dtype),
        grid_spec=grid_spec,
        input_output_aliases={4: 0},
        compiler_params=_cparams("arbitrary"),
        name="moe_dispatch",
    )(pad_start, eid, rank, h2, jnp.zeros((n_rows, d), h2.dtype))


def _expert_kernel(be_ref, nu_ref, x_ref, w13_ref, w2_ref, y_ref):
    del be_ref
    i = pl.program_id(0)
    f = w2_ref.shape[0]

    @pl.when(i < nu_ref[0])
    def _():
        x_lo, x_hi = _unpack_halves(x_ref[...])
        n = x_lo.shape[1]
        h13 = _dot(x_lo.astype(BF16), w13_ref[0:n, :]) + _dot(x_hi.astype(BF16), w13_ref[n:2 * n, :])
        h1 = h13[:, :f]
        hb = (h1 * _sigmoid(h1)) * h13[:, f:]
        y_ref[...] = _pack_halves(_dot(hb.astype(BF16), w2_ref[...]))

    @pl.when(i >= nu_ref[0])
    def _():
        y_ref[...] = jnp.zeros_like(y_ref)


def _moe_experts(blk_eid, n_used, xin, w13, w2):
    n_rows, dh = xin.shape
    e, f, d = w2.shape
    grid_spec = pltpu.PrefetchScalarGridSpec(
        num_scalar_prefetch=2,
        grid=(n_rows // MOE_BLOCK,),
        in_specs=[pl.BlockSpec((MOE_BLOCK, dh), lambda i, be, nu: (i, 0)),
                  pl.BlockSpec((None, d, 2 * f), lambda i, be, nu: (be[i], 0, 0)),
                  pl.BlockSpec((None, f, d), lambda i, be, nu: (be[i], 0, 0))],
        out_specs=pl.BlockSpec((MOE_BLOCK, dh), lambda i, be, nu: (i, 0)),
    )
    return pl.pallas_call(
        _expert_kernel,
        out_shape=jax.ShapeDtypeStruct((n_rows, dh), U32),
        grid_spec=grid_spec,
        compiler_params=_cparams("arbitrary"),
        name="moe_experts",
    )(blk_eid, n_used, xin, w13, w2)


def _combine_kernel(ps_ref, eid_ref, rank_ref, wt_ref, y_ref, x1_ref, gt_ref, g_ref, o_ref, buf_ref, sem):
    bb, tt, d = x1_ref.shape
    rows = bb * tt

    def row_copy(k, t, src):
        return pltpu.make_async_copy(y_ref.at[pl.ds(src, 1)], buf_ref.at[k, pl.ds(t, 1)], sem)

    def issue(t, carry):
        for k in range(TOP_K):
            row_copy(k, t, ps_ref[eid_ref[k, t]] + rank_ref[k, t]).start()
        return carry

    def drain(t, carry):
        for k in range(TOP_K):
            row_copy(k, 0, 0).wait()
        return carry

    lax.fori_loop(0, rows, issue, 0, unroll=8)
    lax.fori_loop(0, rows, drain, 0, unroll=8)
    wt = wt_ref[...]
    lo0, hi0 = _unpack_halves(buf_ref[0])
    lo1, hi1 = _unpack_halves(buf_ref[1])
    w0, w1 = wt[:, 0:1], wt[:, 1:2]
    f = jnp.concatenate([lo0 * w0 + lo1 * w1, hi0 * w0 + hi1 * w1], axis=1).reshape(bb, tt, d)
    o_ref[...] = x1_ref[...] + gt_ref[...] * _rms(f, g_ref[...])


def _moe_combine(pad_start, eid, rank, wt_rows, y, x1, gt, g_post):
    b, t, d = x1.shape
    if t >= MOE_BLOCK:
        bb, tt = 1, MOE_BLOCK
    else:
        bb, tt = MOE_BLOCK // t, t
    assert t % tt == 0 and b % bb == 0
    rows = bb * tt
    nt = t // tt
    smem_spec = pl.BlockSpec((TOP_K, rows), lambda ib, it, ps: (0, ib * nt + it), memory_space=pltpu.SMEM)
    grid_spec = pltpu.PrefetchScalarGridSpec(
        num_scalar_prefetch=1,
        grid=(b // bb, nt),
        in_specs=[smem_spec, smem_spec,
                  pl.BlockSpec((rows, TOP_K), lambda ib, it, ps: (ib * nt + it, 0)),
                  pl.BlockSpec(memory_space=pl.ANY),
                  pl.BlockSpec((bb, tt, d), lambda ib, it, ps: (ib, it, 0)),
                  pl.BlockSpec((bb, 1, d), lambda ib, it, ps: (ib, 0, 0)),
                  pl.BlockSpec((1, 1, d), lambda ib, it, ps: (0, 0, 0))],
        out_specs=pl.BlockSpec((bb, tt, d), lambda ib, it, ps: (ib, it, 0)),
        scratch_shapes=[pltpu.VMEM((TOP_K, rows, d // 2), U32), pltpu.SemaphoreType.DMA(())],
    )
    return pl.pallas_call(
        _combine_kernel,
        out_shape=jax.ShapeDtypeStruct((b, t, d), F32),
        grid_spec=grid_spec,
        compiler_params=_cparams("arbitrary", "arbitrary"),
        name="moe_combine",
    )(pad_start, eid, rank, wt_rows, y, x1, gt, g_post.reshape(1, 1, d))


def _hier_moe(h2, eid, wt, x1, gt_f, g_post, w13, w2):
    n = h2.shape[0]
    rank, cnt = _moe_rank(eid)
    counts = cnt[:, 0].astype(I32)
    padded = (counts + MOE_BLOCK - 1) // MOE_BLOCK * MOE_BLOCK
    pad_end = jnp.cumsum(padded)
    pad_start = (pad_end - padded).astype(I32)
    n_blocks = (n * TOP_K + N_EXPERTS * (MOE_BLOCK - 1) + MOE_BLOCK - 1) // MOE_BLOCK
    blk_start = jnp.arange(n_blocks, dtype=I32) * MOE_BLOCK
    blk_eid = jnp.minimum(jnp.sum((pad_end[None, :] <= blk_start[:, None]).astype(I32), axis=1), N_EXPERTS - 1)
    n_used = (pad_end[-1:] // MOE_BLOCK).astype(I32)
    xin = _moe_dispatch(pad_start, eid, rank, h2, n_blocks * MOE_BLOCK)
    y = _moe_experts(blk_eid, n_used, xin, w13, w2)
    return _moe_combine(pad_start, eid, rank, jnp.swapaxes(wt, 0, 1), y, x1, gt_f, g_post)


def _layer(x, mods, shift_prev, rkv_last, s0, cache, p, w):
    b, t, d = x.shape
    ca = p["w0"].shape[-1]
    sh_a, sc_a, gt_a, sh_f, sc_f, gt_f = mods
    r, k, v, q, kb, vb = _in_proj(x, sc_a, sh_a, p["g_pre_mix"], w["w_main"], [F32, F32, F32, BF16, F32, F32],
                                  [1.0, 1.0, 1.0, Q_PRESCALE, 1.0, 1.0])
    wl, a, gate, logf, shift_new = _lora_heads(x, sc_a, sh_a, p["g_pre_mix"], shift_prev, p)
    ya, s_new = _rwkv7(r, k, v, wl, a, gate, rkv_last[..., :ca], rkv_last[..., ca:2 * ca], rkv_last[..., 2 * ca:],
                       s0, p)
    if cache is None:
        fq, _ = _forget_layouts(logf, None)
        yb = _fox_prompt(q, kb, vb, fq)
    else:
        k_past, v_past, logf_past = cache
        past = k_past.shape[1]
        fq, fk = _forget_layouts(logf, logf_past)
        yb = _fox_sample(q, kb, vb, k_past.reshape(b, past, -1).astype(BF16),
                         v_past.reshape(b, past, -1).astype(BF16), fq, fk)
    x1, h2, eid, wt = _postmix(ya, yb, x, gt_a, sc_f, sh_f, p["g_post_mix"], p["g_pre_ffn"],
                               w["wo_a"], w["wo_b"], w["wr_hi"], w["wr_lo"], w["br"])
    out = _hier_moe(h2, eid, wt, x1, gt_f, p["g_post_ffn"], w["w13"], w["w2"])
    nh = logf.shape[-1]
    return (out, shift_new.reshape(b, d), _pairs_to_state(s_new),
            kb.reshape(b, t, nh, HEAD_DIM), vb.reshape(b, t, nh, HEAD_DIM), logf)


def kernel(x_prompt, x_sample, state_shift, state_wkv, cache_k, cache_v, cache_logf, c_prompt, c_sample, w_ada, b_ada, g_pre_mix, g_post_mix, g_pre_ffn, g_post_ffn, w_in, b_f, mu_rkv, mu_wag, w0, w_lora_a, w_lora_b, a0, a_lora_a, a_lora_b, g_lora_a, g_lora_b, k_k, k_a, r_k, ln_x_w, ln_x_b, w_out, w_rg, b_rg, w_re, b_re, w1, w3, w2):
    stacked = dict(w_ada=w_ada, b_ada=b_ada, g_pre_mix=g_pre_mix, g_post_mix=g_post_mix, g_pre_ffn=g_pre_ffn,
                   g_post_ffn=g_post_ffn, w_in=w_in, b_f=b_f, mu_rkv=mu_rkv, mu_wag=mu_wag, w0=w0,
                   w_lora_a=w_lora_a, w_lora_b=w_lora_b, a0=a0, a_lora_a=a_lora_a, a_lora_b=a_lora_b,
                   g_lora_a=g_lora_a, g_lora_b=g_lora_b, k_k=k_k, k_a=k_a, r_k=r_k, ln_x_w=ln_x_w,
                   ln_x_b=ln_x_b, w_out=w_out, w_rg=w_rg, b_rg=b_rg, w_re=w_re, b_re=b_re, w1=w1, w3=w3, w2=w2)
    depth = w_ada.shape[0]
    assert depth == 1, "one layer per call"
    p = {name: arr[0] for name, arr in stacked.items()}
    bp, _, d = x_prompt.shape
    bs = x_sample.shape[0]
    ca = p["w0"].shape[-1]
    n_main = 3 * ca + 3 * (d - ca)
    p["w_forget"] = p["w_in"][:, n_main:]

    wr = _pad_to(jnp.concatenate([p["w_re"], p["w_rg"]], axis=1).T, 0, ROUTER_ROWS)
    wr_hi = wr.astype(BF16)
    w = dict(
        w_main=p["w_in"][:, :n_main].astype(BF16),
        wo_a=p["w_out"][:ca].astype(BF16), wo_b=p["w_out"][ca:].astype(BF16),
        wr_hi=wr_hi, wr_lo=(wr - wr_hi.astype(F32)).astype(BF16),
        br=_pad_to(jnp.concatenate([p["b_re"], p["b_rg"]]), 0, ROUTER_ROWS).reshape(ROUTER_ROWS, 1),
        w13=jnp.concatenate([p["w1"], p["w3"]], axis=-1).astype(BF16), w2=p["w2"].astype(BF16),
    )

    mod = _ada_mod(jnp.concatenate([c_prompt, c_sample], axis=0), p["w_ada"], p["b_ada"])
    mods = [mod[:, None, i * d:(i + 1) * d] for i in range(6)]
    mods_p = [m[:bp] for m in mods]
    mods_s = [m[bp:] for m in mods]

    n_pairs = ca // PAIR
    out_p = _layer(x_prompt, mods_p, jnp.zeros((bp, 1, d), F32), jnp.zeros((bp, 1, 3 * ca), F32),
                   jnp.zeros((bp, n_pairs, PAIR, PAIR), F32), None, p, w)
    rkv_last = _matmul_rows(state_shift[0], w["w_main"][:, :3 * ca], tn=ca)[:, None, :]
    out_s = _layer(x_sample, mods_s, state_shift[0][:, None, :], rkv_last, _state_to_pairs(state_wkv[0]),
                   (cache_k[0], cache_v[0], cache_logf[0]), p, w)
    yp, shp, wkvp, kp, vp, lfp = out_p
    ys, shs, wkvs, ks_, vs_, lfs = out_s
    return (yp, ys, shp[None], wkvp[None], kp[None], vp[None], lfp[None],
            shs[None], wkvs[None], ks_[None], vs_[None], lfs[None])
```

```python
import functools
import math

import jax
import jax.numpy as jnp
from jax import lax
from jax.experimental import pallas as pl
from jax.experimental.pallas import tpu as pltpu

F32 = jnp.float32
BF16 = jnp.bfloat16
I32 = jnp.int32

HEAD_DIM = 64
LANES = 128
PAIR = 2 * HEAD_DIM
RMS_EPS = 1e-6
GN_EPS = 64e-5
N_GROUPS = 4
EXPERTS_PER_GROUP = 8
N_EXPERTS = N_GROUPS * EXPERTS_PER_GROUP
TOP_K = 2
CHUNK = 64
VMEM_LIMIT_BYTES = 56 * 1024 * 1024
ROW_TILE = 512
MOE_BLOCK = 256


def _cparams(*sem):
    return pltpu.CompilerParams(dimension_semantics=sem, vmem_limit_bytes=VMEM_LIMIT_BYTES)


def _dot(a, b):
    return jnp.dot(a, b, preferred_element_type=F32)


def _dot_nt(a, b):
    return lax.dot_general(a, b, (((1,), (1,)), ((), ())), preferred_element_type=F32)


def _dot_tn(a, b):
    return lax.dot_general(a, b, (((0,), (0,)), ((), ())), preferred_element_type=F32)


def _split3(x):
    hi = x.astype(BF16)
    r1 = x - hi.astype(F32)
    mid = r1.astype(BF16)
    lo = (r1 - mid.astype(F32)).astype(BF16)
    return hi, mid, lo


def _split2(x):
    hi = x.astype(BF16)
    return hi, (x - hi.astype(F32)).astype(BF16)


def _dot3_exact_rhs(x, m):
    hi, mid, lo = _split3(x)
    return _dot(hi, m) + _dot(mid, m) + _dot(lo, m)


def _sigmoid(x):
    return 1.0 / (1.0 + jnp.exp(-x))


def _log_sigmoid(x):
    return jnp.minimum(x, 0.0) - jnp.log(1.0 + jnp.exp(-jnp.abs(x)))


def _mod_norm(x, g, scale, shift):
    y = x * lax.rsqrt(jnp.mean(x * x, axis=-1, keepdims=True) + RMS_EPS)
    return (y * g) * (1.0 + scale) + shift


def _row_tiling(batch, seq):
    if seq >= ROW_TILE:
        assert seq % ROW_TILE == 0
        return 1, ROW_TILE
    assert ROW_TILE % seq == 0 and batch % (ROW_TILE // seq) == 0
    return ROW_TILE // seq, seq


def _ada_kernel(c_ref, w_ref, b_ref, o_ref):
    c = c_ref[...]
    s = c * _sigmoid(c)
    s_hi, s_lo = _split2(s)
    w_hi, w_lo = _split2(w_ref[...])
    o_ref[...] = (_dot(s_hi, w_hi) + _dot(s_lo, w_hi) + _dot(s_hi, w_lo)) + b_ref[...]


def _ada_mod(c_all, w_ada, b_ada):
    m, d = c_all.shape
    n = w_ada.shape[1]
    tn = 1024
    return pl.pallas_call(
        _ada_kernel,
        out_shape=jax.ShapeDtypeStruct((m, n), F32),
        grid=(n // tn,),
        in_specs=[pl.BlockSpec((m, d), lambda j: (0, 0)),
                  pl.BlockSpec((d, tn), lambda j: (0, j)),
                  pl.BlockSpec((1, tn), lambda j: (0, j))],
        out_specs=pl.BlockSpec((m, tn), lambda j: (0, j)),
        compiler_params=_cparams("arbitrary"),
        name="ada_mod",
    )(c_all, w_ada, b_ada.reshape(1, n))


def _proj_kernel(x_ref, sc_ref, sh_ref, g_ref, w_ref, *refs, out_scales):
    outs, hs_ref = refs[:-1], refs[-1]
    j = pl.program_id(2)
    bb, tt, d = x_ref.shape

    @pl.when(j == 0)
    def _():
        h = _mod_norm(x_ref[...], g_ref[...], sc_ref[...], sh_ref[...])
        hs_ref[...] = h.reshape(bb * tt, d).astype(BF16)

    acc = _dot(hs_ref[...], w_ref[...])
    for idx, o_ref in enumerate(outs):
        @pl.when(j == idx)
        def _(o_ref=o_ref, scale=out_scales[idx]):
            scaled = acc if scale == 1.0 else acc * scale
            o_ref[...] = scaled.reshape(o_ref.shape).astype(o_ref.dtype)


def _in_proj(x, sc, sh, g, w_main, out_dtypes, out_scales):
    b, t, d = x.shape
    n_out = len(out_dtypes)
    cw = w_main.shape[1] // n_out
    bb, tt = _row_tiling(b, t)
    row_spec = pl.BlockSpec((bb, tt, d), lambda ib, it, j: (ib, it, 0))
    mod_spec = pl.BlockSpec((bb, 1, d), lambda ib, it, j: (ib, 0, 0))
    out_spec = pl.BlockSpec((bb, tt, cw), lambda ib, it, j: (ib, it, 0))
    return pl.pallas_call(
        functools.partial(_proj_kernel, out_scales=tuple(out_scales)),
        out_shape=[jax.ShapeDtypeStruct((b, t, cw), dt) for dt in out_dtypes],
        grid=(b // bb, t // tt, n_out),
        in_specs=[row_spec, mod_spec, mod_spec,
                  pl.BlockSpec((1, 1, d), lambda ib, it, j: (0, 0, 0)),
                  pl.BlockSpec((d, cw), lambda ib, it, j: (0, j))],
        out_specs=[out_spec] * n_out,
        scratch_shapes=[pltpu.VMEM((bb * tt, d), BF16)],
        compiler_params=_cparams("arbitrary", "arbitrary", "arbitrary"),
        name="in_proj",
    )(x, sc, sh, g.reshape(1, 1, d), w_main)


def _lora_kernel(x_ref, sc_ref, sh_ref, g_ref, prev_ref, muw_ref, mua_ref, mug_ref,
                 aw_ref, bw_ref, w0_ref, aa_ref, ba_ref, a0_ref, ag_ref, bg_ref, wf_ref, bf_ref,
                 wl_ref, a_ref, gate_ref, lf_ref, shn_ref, carry_ref):
    it = pl.program_id(1)
    bb, tt, d = x_ref.shape
    rows = bb * tt
    h = _mod_norm(x_ref[...], g_ref[...], sc_ref[...], sh_ref[...])

    @pl.when(it == 0)
    def _():
        carry_ref[...] = prev_ref[...]

    rolled = pltpu.roll(h.reshape(rows, d), 1, 0).reshape(bb, tt, d)
    tpos = lax.broadcasted_iota(I32, (bb, tt, d), 1)
    h_prev = jnp.where(tpos == 0, carry_ref[...], rolled)
    last = h[:, tt - 1:tt, :]
    carry_ref[...] = last
    shn_ref[...] = last

    xx = h_prev - h

    def mix(mu_ref):
        return (h + xx * mu_ref[...]).reshape(rows, d).astype(BF16)

    ca = wl_ref.shape[-1]
    lw = jnp.tanh(_dot(mix(muw_ref), aw_ref[...]))
    w = w0_ref[...] + _dot(lw.astype(BF16), bw_ref[...])
    wl_ref[...] = (-math.exp(-0.5) * _sigmoid(w)).reshape(bb, tt, ca)
    la = _dot(mix(mua_ref), aa_ref[...])
    a_ref[...] = _sigmoid(a0_ref[...] + _dot(la.astype(BF16), ba_ref[...])).reshape(bb, tt, ca)
    lg = _sigmoid(_dot(mix(mug_ref), ag_ref[...]))
    gate_ref[...] = _dot(lg.astype(BF16), bg_ref[...]).reshape(bb, tt, ca).astype(gate_ref.dtype)
    nh = lf_ref.shape[-1]
    z = _dot(h.reshape(rows, d).astype(BF16), wf_ref[...]) + bf_ref[...]
    lf_ref[...] = _log_sigmoid(z)[:, :nh].reshape(bb, tt, nh)


def _pad_to(x, axis, size):
    pad = [(0, 0)] * x.ndim
    pad[axis] = (0, size - x.shape[axis])
    return jnp.pad(x, pad)


def _round_up(n, m):
    return (n + m - 1) // m * m


def _lora_heads(x, sc, sh, g, shift_prev, p):
    b, t, d = x.shape
    ca = p["w0"].shape[-1]
    nh = p["b_f"].shape[-1]
    bb, tt = _row_tiling(b, t)

    def lora_pair(a_w, b_w):
        r = _round_up(a_w.shape[1], LANES)
        return _pad_to(a_w, 1, r).astype(BF16), _pad_to(b_w, 0, r).astype(BF16)

    aw, bw = lora_pair(p["w_lora_a"], p["w_lora_b"])
    aa, ba = lora_pair(p["a_lora_a"], p["a_lora_b"])
    ag, bg = lora_pair(p["g_lora_a"], p["g_lora_b"])
    wf = _pad_to(p["w_forget"], 1, LANES).astype(BF16)
    bf = _pad_to(p["b_f"].reshape(1, nh), 1, LANES)
    mu = p["mu_wag"].reshape(3, 1, 1, d)

    row_spec = pl.BlockSpec((bb, tt, d), lambda ib, it: (ib, it, 0))
    mod_spec = pl.BlockSpec((bb, 1, d), lambda ib, it: (ib, 0, 0))
    vec_spec = pl.BlockSpec((1, 1, d), lambda ib, it: (0, 0, 0))

    def full(a):
        return pl.BlockSpec(a.shape, lambda ib, it: (0,) * a.ndim)

    ca_spec = pl.BlockSpec((bb, tt, ca), lambda ib, it: (ib, it, 0))
    weights = [aw, bw, p["w0"].reshape(1, ca), aa, ba, p["a0"].reshape(1, ca), ag, bg, wf, bf]
    return pl.pallas_call(
        _lora_kernel,
        out_shape=[jax.ShapeDtypeStruct((b, t, ca), F32), jax.ShapeDtypeStruct((b, t, ca), F32),
                   jax.ShapeDtypeStruct((b, t, ca), BF16), jax.ShapeDtypeStruct((b, t, nh), F32),
                   jax.ShapeDtypeStruct((b, 1, d), F32)],
        grid=(b // bb, t // tt),
        in_specs=[row_spec, mod_spec, mod_spec, vec_spec, mod_spec, vec_spec, vec_spec, vec_spec]
                 + [full(a) for a in weights],
        out_specs=[ca_spec, ca_spec, ca_spec,
                   pl.BlockSpec((bb, tt, nh), lambda ib, it: (ib, it, 0)),
                   pl.BlockSpec((bb, 1, d), lambda ib, it: (ib, 0, 0))],
        scratch_shapes=[pltpu.VMEM((bb, 1, d), F32)],
        compiler_params=_cparams("arbitrary", "arbitrary"),
        name="lora_heads",
    )(x, sc, sh, g.reshape(1, 1, d), shift_prev, mu[0], mu[1], mu[2], *weights)


def _cumsum_kernel(x_ref, o_ref, carry_ref):
    it = pl.program_id(1)
    bb, nh, tl = x_ref.shape

    @pl.when(it == 0)
    def _():
        carry_ref[...] = jnp.zeros_like(carry_ref)

    r = lax.broadcasted_iota(I32, (tl, tl), 0)
    c = lax.broadcasted_iota(I32, (tl, tl), 1)
    upper = jnp.where(r <= c, 1.0, 0.0).astype(BF16)
    cs = _dot3_exact_rhs(x_ref[...].reshape(bb * nh, tl), upper) + carry_ref[...]
    o_ref[...] = cs.reshape(bb, nh, tl)
    carry_ref[...] = cs[:, tl - 1:tl]


def _cumsum_lanes(x):
    b, nh, l = x.shape
    tl = l if l <= 2304 else 1024
    assert l % tl == 0 and nh % 8 == 0
    bb = math.gcd(b, 8)
    return pl.pallas_call(
        _cumsum_kernel,
        out_shape=jax.ShapeDtypeStruct((b, nh, l), F32),
        grid=(b // bb, l // tl),
        in_specs=[pl.BlockSpec((bb, nh, tl), lambda ib, it: (ib, 0, it))],
        out_specs=pl.BlockSpec((bb, nh, tl), lambda ib, it: (ib, 0, it)),
        scratch_shapes=[pltpu.VMEM((bb * nh, 1), F32)],
        compiler_params=_cparams("arbitrary", "arbitrary"),
        name="logf_cumsum",
    )(x)


NEG_BIG = -1e30


def _head_masked(q):
    lane = lax.broadcasted_iota(I32, q.shape, 1)
    first = lane < HEAD_DIM
    zero = jnp.zeros_like(q)
    return first, (jnp.where(first, q, zero), jnp.where(first, zero, q))


FOX_TILE = 512
LOG2E = 1.0 / math.log(2.0)
Q_PRESCALE = HEAD_DIM ** -0.5 * LOG2E


def _with_bias_lanes(x, first, f_col, own_sign):
    lane = lax.broadcasted_iota(I32, x.shape, 1)
    out = []
    for e in range(2):
        keep = first if e == 0 else jnp.logical_not(first)
        base = HEAD_DIM if e == 0 else 0
        terms = [t.astype(F32) * own_sign for t in _split3(f_col[:, e:e + 1])]
        ones = jnp.ones_like(terms[0])
        cols = terms + [ones] * 3 if own_sign < 0 else [ones] * 3 + terms
        y = jnp.where(keep, x, 0.0)
        for i, col in enumerate(cols):
            y = jnp.where(lane == base + i, col, y)
        out.append(y.astype(BF16))
    return out


def _fox_prompt_kernel(q_ref, k_ref, v_ref, f_ref, o_ref, k0_ref, k1_ref, vt_ref):
    t = k_ref.shape[0]
    tq = tk = FOX_TILE
    half = HEAD_DIM
    first_t = lax.broadcasted_iota(I32, (t, PAIR), 1) < half
    kaug = _with_bias_lanes(k_ref[...], first_t, f_ref[...], -1.0)
    k0_ref[...] = kaug[0]
    k1_ref[...] = kaug[1]
    kaug_refs = (k0_ref, k1_ref)
    for i in range(t // tk):
        vt_ref[:, i * tk:(i + 1) * tk] = v_ref[i * tk:(i + 1) * tk, :].T.astype(BF16)

    first_q = lax.broadcasted_iota(I32, (tq, PAIR), 1) < half
    krow = lax.broadcasted_iota(I32, (tk, tq), 0)
    qcol = lax.broadcasted_iota(I32, (tk, tq), 1)

    def q_block(qi, carry):
        q0 = pl.multiple_of(qi * tq, tq)
        qaug = _with_bias_lanes(q_ref[pl.ds(q0, tq), :].astype(F32), first_q, f_ref[pl.ds(q0, tq), :], 1.0)

        def kv_blocks(state, blocks):
            starts = [pl.multiple_of(k0, tk) for k0, _ in blocks]
            s = [[_dot_nt(kaug_refs[e][pl.ds(k0, tk), :], qaug[e]) for e in range(2)] for k0 in starts]
            state = list(state)
            for (_, masked), k0, s_blk in zip(blocks, starts, s):
                for e in range(2):
                    m_prev, l_prev, acc = state[e]
                    s_e = jnp.where(krow <= qcol, s_blk[e], NEG_BIG) if masked else s_blk[e]
                    m_new = jnp.maximum(m_prev, jnp.max(s_e, axis=0, keepdims=True))
                    p = jnp.exp2(s_e - m_new)
                    alpha = jnp.exp2(m_prev - m_new)
                    l_new = alpha * l_prev + jnp.sum(p, axis=0, keepdims=True)
                    pv = _dot(vt_ref[e * half:(e + 1) * half, pl.ds(k0, tk)], p.astype(BF16))
                    state[e] = (m_new, l_new, acc * alpha + pv)
            return tuple(state)

        neg = jnp.full((1, tq), NEG_BIG, F32)
        zero = jnp.zeros((1, tq), F32)
        init = tuple((neg, zero, jnp.zeros((half, tq), F32)) for _ in range(2))
        state = lax.fori_loop(
            0, qi // 2, lambda i, st: kv_blocks(st, [(2 * i * tk, False), ((2 * i + 1) * tk, False)]), init)
        state = lax.cond(
            qi % 2 == 1,
            lambda st: kv_blocks(st, [((qi - 1) * tk, False), (qi * tk, True)]),
            lambda st: kv_blocks(st, [(qi * tk, True)]),
            state)
        o_t = jnp.concatenate([state[e][2] / state[e][1] for e in range(2)], axis=0)
        o_ref[pl.ds(q0, tq), :] = o_t.T.astype(o_ref.dtype)
        return carry

    lax.fori_loop(0, t // tq, q_block, 0)


def _fox_prompt(q, k, v, f_rows):
    b, t, cb = q.shape
    hp = cb // PAIR
    assert t % FOX_TILE == 0
    seq_spec = pl.BlockSpec((None, t, PAIR), lambda ib, ih: (ib, 0, ih))
    return pl.pallas_call(
        _fox_prompt_kernel,
        out_shape=jax.ShapeDtypeStruct((b, t, cb), BF16),
        grid=(b, hp),
        in_specs=[seq_spec, seq_spec, seq_spec,
                  pl.BlockSpec((None, None, t, 2), lambda ib, ih: (ib, ih, 0, 0))],
        out_specs=seq_spec,
        scratch_shapes=[pltpu.VMEM((t, PAIR), BF16), pltpu.VMEM((t, PAIR), BF16), pltpu.VMEM((PAIR, t), BF16)],
        compiler_params=_cparams("arbitrary", "arbitrary"),
        name="fox_prompt",
    )(q, k, v, f_rows)


def _fox_sample_kernel(q_ref, kn_ref, vn_ref, kc_ref, vc_ref, fq_ref, fk_ref, o_ref):
    ts = q_ref.shape[0]
    past = kc_ref.shape[0]
    n_pairs = q_ref.shape[1] // PAIR
    row = lax.broadcasted_iota(I32, (ts, ts), 0)
    col = lax.broadcasted_iota(I32, (ts, ts), 1)
    for pp in range(n_pairs):
        lanes = slice(pp * PAIR, (pp + 1) * PAIR)
        first, qs = _head_masked(q_ref[:, lanes])
        kc = kc_ref[:, lanes]
        vc = vc_ref[:, lanes]
        kn = kn_ref[:, lanes].astype(BF16)
        vn = vn_ref[:, lanes].astype(BF16)
        outs = []
        for e in range(2):
            fq = fq_ref[pp, :, e:e + 1]
            s_p = _dot_nt(qs[e], kc) + (fq - fk_ref[pp, e:e + 1, 0:past])
            s_n = _dot_nt(qs[e], kn) + (fq - fk_ref[pp, e:e + 1, past:past + ts])
            s_n = jnp.where(col <= row, s_n, NEG_BIG)
            m = jnp.maximum(jnp.max(s_p, axis=1, keepdims=True), jnp.max(s_n, axis=1, keepdims=True))
            p_p = jnp.exp2(s_p - m)
            p_n = jnp.exp2(s_n - m)
            l = jnp.sum(p_p, axis=1, keepdims=True) + jnp.sum(p_n, axis=1, keepdims=True)
            outs.append((_dot(p_p.astype(BF16), vc) + _dot(p_n.astype(BF16), vn)) / l)
        o_ref[:, lanes] = jnp.where(first, outs[0], outs[1]).astype(o_ref.dtype)


def _fox_sample(q, kn, vn, kc, vc, fq, fk):
    b, ts, cb = q.shape
    past = kc.shape[1]
    lp = fk.shape[-1]
    lb = min(cb, 4 * PAIR)
    npb = lb // PAIR
    new_spec = pl.BlockSpec((None, ts, lb), lambda ib, ig: (ib, 0, ig))
    past_spec = pl.BlockSpec((None, past, lb), lambda ib, ig: (ib, 0, ig))
    return pl.pallas_call(
        _fox_sample_kernel,
        out_shape=jax.ShapeDtypeStruct((b, ts, cb), BF16),
        grid=(b, cb // lb),
        in_specs=[new_spec, new_spec, new_spec, past_spec, past_spec,
                  pl.BlockSpec((None, npb, ts, 2), lambda ib, ig: (ib, ig, 0, 0)),
                  pl.BlockSpec((None, npb, 2, lp), lambda ib, ig: (ib, ig, 0, 0))],
        out_specs=new_spec,
        compiler_params=_cparams("arbitrary", "arbitrary"),
        name="fox_sample",
    )(q, kn, vn, kc, vc, fq, fk)


def _forget_layouts(logf_new, logf_past):
    b, t, nh = logf_new.shape
    lt = jnp.swapaxes(logf_new, 1, 2)
    past = 0
    if logf_past is not None:
        past = logf_past.shape[1]
        lt = jnp.concatenate([jnp.swapaxes(logf_past, 1, 2), lt], axis=2)
    lp = _round_up(past + t, LANES)
    ft = _cumsum_lanes(_pad_to(lt, 2, lp)) * LOG2E
    fk = ft.reshape(b, nh // 2, 2, lp)
    fq = jnp.swapaxes(fk[..., past:past + t], 2, 3)
    return fq, fk


def _rwkv_kernel(r_ref, k_ref, v_ref, wl_ref, a_ref, g_ref, rl_ref, kl_ref, vl_ref, s0_ref,
                 mur_ref, muk_ref, muv_ref, kk_ref, ka_ref, rk_ref, lnw_ref, lnb_ref,
                 y_ref, sout_ref, st_ref, prev_ref):
    itb = pl.program_id(2)
    tb, lanes = r_ref.shape
    n_pairs = lanes // PAIR
    c = min(CHUNK, tb)

    pairs = range(n_pairs)
    heads = range(2)

    @pl.when(itb == 0)
    def _():
        zero = jnp.zeros((HEAD_DIM, HEAD_DIM), F32)
        for p in pairs:
            st_ref[p] = jnp.concatenate([jnp.concatenate([s0_ref[2 * p], zero], axis=1),
                                         jnp.concatenate([zero, s0_ref[2 * p + 1]], axis=1)], axis=0)
        prev_ref[0:1, :] = rl_ref[...]
        prev_ref[1:2, :] = kl_ref[...]
        prev_ref[2:3, :] = vl_ref[...]
    ri = lax.broadcasted_iota(I32, (c, 2 * c), 0)
    ci = lax.broadcasted_iota(I32, (c, 2 * c), 1) % c
    strict2 = ri > ci
    incl2 = ri >= ci
    tri_ones = jnp.where(incl2[:, :c], 1.0, 0.0).astype(BF16)
    li = lax.broadcasted_iota(I32, (PAIR, PAIR), 0) // HEAD_DIM
    lj = lax.broadcasted_iota(I32, (PAIR, PAIR), 1) // HEAD_DIM
    same_head = li == lj
    head_ones = jnp.where(same_head, 1.0, 0.0).astype(BF16)
    first = lax.broadcasted_iota(I32, (c, PAIR), 1) < HEAD_DIM
    row0 = lax.broadcasted_iota(I32, (c, PAIR), 0) == 0
    zeros_cb = jnp.zeros((c, PAIR), BF16)

    def head_sum2(x):
        hi, lo = _split2(x)
        s = _dot(jnp.concatenate([hi, lo], axis=0), head_ones)
        return s[:c] + s[c:]

    def shifted(x, prev_row):
        return jnp.where(row0, prev_row, pltpu.roll(x, 1, 0))

    def by_head(x):
        return [jnp.where(first, x, zeros_cb), jnp.where(first, zeros_cb, x)]

    def chunk(ic, carry):
        rows = pl.ds(pl.multiple_of(ic * c, c), c)
        lns = [slice(pp * PAIR, (pp + 1) * PAIR) for pp in pairs]
        r, k, v, wl, a, kk = [], [], [], [], [], []
        for ln in lns:
            r_raw, k_raw, v_raw = r_ref[rows, ln], k_ref[rows, ln], v_ref[rows, ln]
            r.append(r_raw + mur_ref[:, ln] * (shifted(r_raw, prev_ref[0:1, ln]) - r_raw))
            k.append(k_raw + muk_ref[:, ln] * (shifted(k_raw, prev_ref[1:2, ln]) - k_raw))
            v.append(v_raw + muv_ref[:, ln] * (shifted(v_raw, prev_ref[2:3, ln]) - v_raw))
            prev_ref[0:1, ln] = r_raw[c - 1:c, :]
            prev_ref[1:2, ln] = k_raw[c - 1:c, :]
            prev_ref[2:3, ln] = v_raw[c - 1:c, :]
            wl.append(wl_ref[rows, ln])
            a.append(a_ref[rows, ln])
            kk.append(k[-1] * kk_ref[:, ln])
        lcum = []
        for p in pairs:
            s = _dot(tri_ones, jnp.concatenate(_split3(wl[p]), axis=1))
            lcum.append(s[:, :PAIR] + s[:, PAIR:2 * PAIR] + s[:, 2 * PAIR:])
        kk_ss = [head_sum2(x * x) for x in kk]
        kf = [k[p] * (1.0 + (a[p] - 1.0) * ka_ref[:, lns[p]]) for p in pairs]
        bonus_s = [_dot((r[p] * kf[p] * rk_ref[:, lns[p]]).astype(BF16), head_ones) for p in pairs]
        kk = [kk[p] / jnp.maximum(jnp.sqrt(kk_ss[p]), 1e-12) for p in pairs]
        b = [kk[p] * a[p] for p in pairs]
        inv = [jnp.exp(-lcum[p]) for p in pairs]
        ltot = [lcum[p][c - 1:c, :] for p in pairs]
        suffix = [jnp.exp(ltot[p] - lcum[p]) for p in pairs]
        r_bar = [(r[p] * jnp.exp(lcum[p])).astype(BF16) for p in pairs]
        a_bar = [(-kk[p] * jnp.exp(lcum[p] - wl[p])).astype(BF16) for p in pairs]
        bk_bar = [jnp.concatenate([(b[p] * inv[p]).astype(BF16), (kf[p] * inv[p]).astype(BF16)], axis=0)
                  for p in pairs]
        bk_suf = [jnp.concatenate([(b[p] * suffix[p]).astype(BF16), (kf[p] * suffix[p]).astype(BF16)], axis=0)
                  for p in pairs]
        vb = [x.astype(BF16) for x in v]
        st = [st_ref[p] for p in pairs]
        stb = [x.astype(BF16) for x in st]

        prod = [_dot_nt(jnp.concatenate(by_head(a_bar[p]) + by_head(r_bar[p]), axis=0), bk_bar[p]) for p in pairs]
        a_full = [[jnp.where(strict2, prod[p][e * c:(e + 1) * c], 0.0).astype(BF16) for e in heads] for p in pairs]
        r_full = [[jnp.where(incl2, prod[p][(2 + e) * c:(3 + e) * c], 0.0).astype(BF16) for e in heads]
                  for p in pairs]
        on_s = [_dot_nt(jnp.concatenate([a_bar[p], r_bar[p]], axis=0), stb[p]) for p in pairs]
        zv = [jnp.concatenate([zeros_cb, vb[p]], axis=0) for p in pairs]
        rhs_v = [[_dot(a_full[p][e], zv[p]) for e in heads] for p in pairs]
        x = [[on_s[p][:c] + jnp.where(first, rhs_v[p][0], rhs_v[p][1])] * 2 for p in pairs]
        x = [list(xp) for xp in x]
        ap = [[a_full[p][e][:, :c] for e in heads] for p in pairs]
        span = 1
        while True:
            span *= 2
            if span >= c:
                x = [[x[p][e] + _dot(ap[p][e], x[p][e].astype(BF16)) for e in heads] for p in pairs]
                break
            res = [[_dot(ap[p][e], jnp.concatenate([x[p][e].astype(BF16), ap[p][e]], axis=1)) for e in heads]
                   for p in pairs]
            x = [[x[p][e] + res[p][e][:, :PAIR] for e in heads] for p in pairs]
            ap = [[res[p][e][:, PAIR:].astype(BF16) for e in heads] for p in pairs]
        uv = [jnp.concatenate([jnp.where(first, x[p][0], x[p][1]).astype(BF16), vb[p]], axis=0) for p in pairs]

        y_h = [[_dot(r_full[p][e], uv[p]) for e in heads] for p in pairs]
        upd = [_dot_tn(uv[p], bk_suf[p]) for p in pairs]
        for p in pairs:
            st_ref[p] = st[p] * jnp.exp(ltot[p]) + jnp.where(same_head, upd[p], 0.0)

        y = [on_s[p][c:] + jnp.where(first, y_h[p][0], y_h[p][1]) for p in pairs]
        mean = [head_sum2(y[p]) * (1.0 / HEAD_DIM) for p in pairs]
        dy = [y[p] - mean[p] for p in pairs]
        var = [head_sum2(dy[p] * dy[p]) * (1.0 / HEAD_DIM) for p in pairs]
        for p in pairs:
            ln = lns[p]
            yn = dy[p] * lax.rsqrt(var[p] + GN_EPS) * lnw_ref[:, ln] + lnb_ref[:, ln]
            y_ref[rows, ln] = ((yn + bonus_s[p] * v[p]) * g_ref[rows, ln].astype(F32)).astype(y_ref.dtype)
        return carry

    lax.fori_loop(0, tb // c, chunk, 0)

    @pl.when(itb == pl.num_programs(2) - 1)
    def _():
        for p in pairs:
            st = st_ref[p]
            sout_ref[2 * p] = st[:HEAD_DIM, :HEAD_DIM]
            sout_ref[2 * p + 1] = st[HEAD_DIM:, HEAD_DIM:]


def _rwkv7(r, k, v, wl, a, g, r_last, k_last, v_last, s0, p, pairs_per_step=8):
    b, t, ca = r.shape
    n_pairs = ca // PAIR
    pps = min(pairs_per_step, n_pairs)
    lanes = pps * PAIR
    tb = min(t, 256)
    assert t % tb == 0 and tb % min(CHUNK, tb) == 0
    seq_spec = pl.BlockSpec((None, tb, lanes), lambda ib, ig, it: (ib, it, ig))
    row_spec = pl.BlockSpec((None, 1, lanes), lambda ib, ig, it: (ib, 0, ig))
    st_spec = pl.BlockSpec((None, 2 * pps, HEAD_DIM, HEAD_DIM), lambda ib, ig, it: (ib, ig, 0, 0))
    par_spec = pl.BlockSpec((1, lanes), lambda ib, ig, it: (0, ig))
    mu = p["mu_rkv"].reshape(3, 1, ca)
    vecs = [mu[0], mu[1], mu[2], p["k_k"].reshape(1, ca), p["k_a"].reshape(1, ca), p["r_k"].reshape(1, ca),
            p["ln_x_w"].reshape(1, ca), p["ln_x_b"].reshape(1, ca)]
    return pl.pallas_call(
        _rwkv_kernel,
        out_shape=[jax.ShapeDtypeStruct((b, t, ca), BF16), jax.ShapeDtypeStruct(s0.shape, F32)],
        grid=(b, n_pairs // pps, t // tb),
        in_specs=[seq_spec] * 6 + [row_spec] * 3 + [st_spec] + [par_spec] * 8,
        out_specs=[seq_spec, st_spec],
        scratch_shapes=[pltpu.VMEM((pps, PAIR, PAIR), F32), pltpu.VMEM((8, lanes), F32)],
        compiler_params=_cparams("arbitrary", "arbitrary", "arbitrary"),
        name="rwkv7",
    )(r, k, v, wl, a, g, r_last, k_last, v_last, s0, *vecs)


def _matmul_kernel(a_ref, w_ref, o_ref):
    o_ref[...] = _dot(a_ref[...].astype(BF16), w_ref[...])


def _matmul_rows(a, w_bf16, tn=1024):
    m, kd = a.shape
    n = w_bf16.shape[1]
    assert n % tn == 0
    return pl.pallas_call(
        _matmul_kernel,
        out_shape=jax.ShapeDtypeStruct((m, n), F32),
        grid=(n // tn,),
        in_specs=[pl.BlockSpec((m, kd), lambda j: (0, 0)), pl.BlockSpec((kd, tn), lambda j: (0, j))],
        out_specs=pl.BlockSpec((m, tn), lambda j: (0, j)),
        compiler_params=_cparams("arbitrary"),
        name="matmul_rows",
    )(a, w_bf16)


ROUTER_ROWS = 40


def _rms(x, g):
    return x * lax.rsqrt(jnp.mean(x * x, axis=-1, keepdims=True) + RMS_EPS) * g


U32 = jnp.uint32
HIGH_HALF = 0xFFFF0000


def _pack_halves(x):
    n = x.shape[1] // 2
    lo = lax.bitcast_convert_type(x[:, :n].astype(BF16).astype(F32), U32)
    hi = lax.bitcast_convert_type(x[:, n:].astype(BF16).astype(F32), U32)
    return (hi & U32(HIGH_HALF)) | (lo >> U32(16))


def _unpack_halves(w):
    lo = lax.bitcast_convert_type(w << U32(16), F32)
    hi = lax.bitcast_convert_type(w & U32(HIGH_HALF), F32)
    return lo, hi


def _postmix_kernel(ya_ref, yb_ref, x_ref, gt_ref, sc_ref, sh_ref, gpost_ref, gpre_ref,
                    woa_ref, wob_ref, wrh_ref, wrl_ref, br_ref,
                    x1_ref, h2_ref, eid_ref, wt_ref):
    bb, tt, d = x_ref.shape
    rows = bb * tt
    o = (_dot(ya_ref[...].reshape(rows, ya_ref.shape[-1]), woa_ref[...])
         + _dot(yb_ref[...].reshape(rows, yb_ref.shape[-1]), wob_ref[...])).reshape(bb, tt, d)
    x1 = x_ref[...] + gt_ref[...] * _rms(o, gpost_ref[...])
    x1_ref[...] = x1
    h2 = _mod_norm(x1, gpre_ref[...], sc_ref[...], sh_ref[...]).reshape(rows, d)
    h2_ref[...] = _pack_halves(h2)

    hi = h2.astype(BF16)
    lo = (h2 - hi.astype(F32)).astype(BF16)
    logits = (_dot_nt(wrh_ref[...], hi) + _dot_nt(wrh_ref[...], lo) + _dot_nt(wrl_ref[...], hi)) + br_ref[...]
    le = logits[0:N_EXPERTS, :]
    lg = logits[N_EXPERTS:N_EXPERTS + N_GROUPS, :]
    gio = lax.broadcasted_iota(I32, lg.shape, 0)
    gmax = jnp.max(lg, axis=0, keepdims=True)
    gi = jnp.min(jnp.where(lg == gmax, gio, N_GROUPS), axis=0, keepdims=True)
    pg = 1.0 / jnp.sum(jnp.exp(lg - gmax), axis=0, keepdims=True)
    eio = lax.broadcasted_iota(I32, le.shape, 0)
    le1 = jnp.where(eio // EXPERTS_PER_GROUP == gi, le, NEG_BIG)
    m1 = jnp.max(le1, axis=0, keepdims=True)
    i1 = jnp.min(jnp.where(le1 == m1, eio, N_EXPERTS), axis=0, keepdims=True)
    le2 = jnp.where(eio == i1, NEG_BIG, le1)
    m2 = jnp.max(le2, axis=0, keepdims=True)
    i2 = jnp.min(jnp.where(le2 == m2, eio, N_EXPERTS), axis=0, keepdims=True)
    e2 = jnp.exp(m2 - m1)
    den = 1.0 + e2
    eid_ref[0:1, :] = i1
    eid_ref[1:2, :] = i2
    wt_ref[0:1, :] = pg / den
    wt_ref[1:2, :] = pg * e2 / den


def _postmix(ya, yb, x, gt, sc, sh, g_post, g_pre, wo_a, wo_b, wr_hi, wr_lo, br):
    b, t, d = x.shape
    bb, tt = _row_tiling(b, t)
    rows = bb * tt
    nt = t // tt
    n = b * t

    def row_spec(w):
        return pl.BlockSpec((bb, tt, w), lambda ib, it: (ib, it, 0))

    mod_spec = pl.BlockSpec((bb, 1, d), lambda ib, it: (ib, 0, 0))
    vec_spec = pl.BlockSpec((1, 1, d), lambda ib, it: (0, 0, 0))

    def full(a):
        return pl.BlockSpec(a.shape, lambda ib, it: (0,) * a.ndim)

    tok_spec = pl.BlockSpec((TOP_K, rows), lambda ib, it: (0, ib * nt + it))
    weights = [wo_a, wo_b, wr_hi, wr_lo, br]
    return pl.pallas_call(
        _postmix_kernel,
        out_shape=[jax.ShapeDtypeStruct((b, t, d), F32), jax.ShapeDtypeStruct((n, d // 2), U32),
                   jax.ShapeDtypeStruct((TOP_K, n), I32), jax.ShapeDtypeStruct((TOP_K, n), F32)],
        grid=(b // bb, nt),
        in_specs=[row_spec(ya.shape[-1]), row_spec(yb.shape[-1]), row_spec(d), mod_spec, mod_spec, mod_spec,
                  vec_spec, vec_spec] + [full(a) for a in weights],
        out_specs=[row_spec(d), pl.BlockSpec((rows, d // 2), lambda ib, it: (ib * nt + it, 0)), tok_spec, tok_spec],
        compiler_params=_cparams("arbitrary", "arbitrary"),
        name="postmix",
    )(ya, yb, x, gt, sc, sh, g_post.reshape(1, 1, d), g_pre.reshape(1, 1, d), *weights)


def _rank_kernel(eid_ref, rank_ref, cnt_ref, carry_ref):
    i = pl.program_id(0)
    tr = eid_ref.shape[1]

    @pl.when(i == 0)
    def _():
        carry_ref[...] = jnp.zeros_like(carry_ref)

    r = lax.broadcasted_iota(I32, (tr, tr), 0)
    c = lax.broadcasted_iota(I32, (tr, tr), 1)
    upper = jnp.where(r <= c, 1.0, 0.0).astype(BF16)
    eio = lax.broadcasted_iota(I32, (N_EXPERTS, tr), 0)
    base = carry_ref[...]
    for k in range(TOP_K):
        onehot = jnp.where(eio == eid_ref[k:k + 1, :], 1.0, 0.0)
        cum = _dot(onehot.astype(BF16), upper)
        rank = jnp.sum(onehot * (base + cum - onehot), axis=0, keepdims=True)
        rank_ref[k:k + 1, :] = rank.astype(I32)
        base = base + cum[:, tr - 1:tr]
    carry_ref[...] = base
    cnt_ref[...] = jnp.broadcast_to(base, cnt_ref.shape)


def _moe_rank(eid):
    n = eid.shape[1]
    tr = min(n, 512)
    assert n % tr == 0
    return pl.pallas_call(
        _rank_kernel,
        out_shape=[jax.ShapeDtypeStruct((TOP_K, n), I32), jax.ShapeDtypeStruct((N_EXPERTS, LANES), F32)],
        grid=(n // tr,),
        in_specs=[pl.BlockSpec((TOP_K, tr), lambda i: (0, i))],
        out_specs=[pl.BlockSpec((TOP_K, tr), lambda i: (0, i)),
                   pl.BlockSpec((N_EXPERTS, LANES), lambda i: (0, 0))],
        scratch_shapes=[pltpu.VMEM((N_EXPERTS, 1), F32)],
        compiler_params=_cparams("arbitrary"),
        name="moe_rank",
    )(eid)


def _dispatch_kernel(dest_ref, h_ref, xin0_ref, xin_ref, sem):
    del xin0_ref
    tt = h_ref.shape[0]

    def row_copy(t, dest):
        return pltpu.make_async_copy(h_ref.at[pl.ds(t, 1)], xin_ref.at[pl.ds(dest, 1)], sem)

    def issue(t, carry):
        for k in range(TOP_K):
            row_copy(t, dest_ref[k, t]).start()
        return carry

    def drain(t, carry):
        for k in range(TOP_K):
            row_copy(0, 0).wait()
        return carry

    lax.fori_loop(0, tt, issue, 0, unroll=8)
    lax.fori_loop(0, tt, drain, 0, unroll=8)


def _moe_dispatch(dest, h2, n_rows):
    n, d = h2.shape
    tt = min(n, MOE_BLOCK)
    return pl.pallas_call(
        _dispatch_kernel,
        out_shape=jax.ShapeDtypeStruct((n_rows, d), h2.dtype),
        grid=(n // tt,),
        in_specs=[pl.BlockSpec((TOP_K, tt), lambda i: (0, i), memory_space=pltpu.SMEM),
                  pl.BlockSpec((tt, d), lambda i: (i, 0)),
                  pl.BlockSpec(memory_space=pl.ANY)],
        out_specs=pl.BlockSpec(memory_space=pl.ANY),
        scratch_shapes=[pltpu.SemaphoreType.DMA(())],
        input_output_aliases={2: 0},
        compiler_params=_cparams("arbitrary"),
        name="moe_dispatch",
    )(dest, h2, jnp.zeros((n_rows, d), h2.dtype))


def _expert_kernel(be_ref, nu_ref, x_ref, w13_ref, w2_ref, y_ref):
    del be_ref
    i = pl.program_id(0)
    f = w2_ref.shape[0]

    @pl.when(i < nu_ref[0])
    def _():
        x_lo, x_hi = _unpack_halves(x_ref[...])
        n = x_lo.shape[1]
        h13 = _dot(x_lo.astype(BF16), w13_ref[0:n, :]) + _dot(x_hi.astype(BF16), w13_ref[n:2 * n, :])
        h1 = h13[:, :f]
        hb = (h1 * _sigmoid(h1)) * h13[:, f:]
        y_ref[...] = _pack_halves(_dot(hb.astype(BF16), w2_ref[...]))

    @pl.when(i >= nu_ref[0])
    def _():
        y_ref[...] = jnp.zeros_like(y_ref)


def _moe_experts(blk_eid, n_used, xin, w13, w2):
    n_rows, dh = xin.shape
    e, f, d = w2.shape
    grid_spec = pltpu.PrefetchScalarGridSpec(
        num_scalar_prefetch=2,
        grid=(n_rows // MOE_BLOCK,),
        in_specs=[pl.BlockSpec((MOE_BLOCK, dh), lambda i, be, nu: (i, 0)),
                  pl.BlockSpec((None, d, 2 * f), lambda i, be, nu: (be[i], 0, 0)),
                  pl.BlockSpec((None, f, d), lambda i, be, nu: (be[i], 0, 0))],
        out_specs=pl.BlockSpec((MOE_BLOCK, dh), lambda i, be, nu: (i, 0)),
    )
    return pl.pallas_call(
        _expert_kernel,
        out_shape=jax.ShapeDtypeStruct((n_rows, dh), U32),
        grid_spec=grid_spec,
        compiler_params=_cparams("arbitrary"),
        name="moe_experts",
    )(blk_eid, n_used, xin, w13, w2)


def _combine_kernel(dest_ref, wt_ref, y_ref, x1_ref, gt_ref, g_ref, o_ref, buf_ref, sem):
    bb, tt, d = x1_ref.shape
    rows = bb * tt

    def row_copy(k, t, src):
        return pltpu.make_async_copy(y_ref.at[pl.ds(src, 1)], buf_ref.at[k, pl.ds(t, 1)], sem)

    def issue(t, carry):
        for k in range(TOP_K):
            row_copy(k, t, dest_ref[k, t]).start()
        return carry

    def drain(t, carry):
        for k in range(TOP_K):
            row_copy(k, 0, 0).wait()
        return carry

    lax.fori_loop(0, rows, issue, 0, unroll=8)
    lax.fori_loop(0, rows, drain, 0, unroll=8)
    wt = wt_ref[...]
    lo0, hi0 = _unpack_halves(buf_ref[0])
    lo1, hi1 = _unpack_halves(buf_ref[1])
    w0, w1 = wt[:, 0:1], wt[:, 1:2]
    f = jnp.concatenate([lo0 * w0 + lo1 * w1, hi0 * w0 + hi1 * w1], axis=1).reshape(bb, tt, d)
    o_ref[...] = x1_ref[...] + gt_ref[...] * _rms(f, g_ref[...])


def _moe_combine(dest, wt_rows, y, x1, gt, g_post):
    b, t, d = x1.shape
    if t >= MOE_BLOCK:
        bb, tt = 1, MOE_BLOCK
    else:
        bb, tt = MOE_BLOCK // t, t
    assert t % tt == 0 and b % bb == 0
    rows = bb * tt
    nt = t // tt
    return pl.pallas_call(
        _combine_kernel,
        out_shape=jax.ShapeDtypeStruct((b, t, d), F32),
        grid=(b // bb, nt),
        in_specs=[pl.BlockSpec((TOP_K, rows), lambda ib, it: (0, ib * nt + it), memory_space=pltpu.SMEM),
                  pl.BlockSpec((rows, TOP_K), lambda ib, it: (ib * nt + it, 0)),
                  pl.BlockSpec(memory_space=pl.ANY),
                  pl.BlockSpec((bb, tt, d), lambda ib, it: (ib, it, 0)),
                  pl.BlockSpec((bb, 1, d), lambda ib, it: (ib, 0, 0)),
                  pl.BlockSpec((1, 1, d), lambda ib, it: (0, 0, 0))],
        out_specs=pl.BlockSpec((bb, tt, d), lambda ib, it: (ib, it, 0)),
        scratch_shapes=[pltpu.VMEM((TOP_K, rows, d // 2), U32), pltpu.SemaphoreType.DMA(())],
        compiler_params=_cparams("arbitrary", "arbitrary"),
        name="moe_combine",
    )(dest, wt_rows, y, x1, gt, g_post.reshape(1, 1, d))


def _hier_moe(h2, eid, wt, x1, gt_f, g_post, w13, w2):
    n = h2.shape[0]
    rank, cnt = _moe_rank(eid)
    counts = cnt[:, 0].astype(I32)
    padded = (counts + MOE_BLOCK - 1) // MOE_BLOCK * MOE_BLOCK
    pad_end = jnp.cumsum(padded)
    pad_start = (pad_end - padded).astype(I32)
    n_blocks = (n * TOP_K + N_EXPERTS * (MOE_BLOCK - 1) + MOE_BLOCK - 1) // MOE_BLOCK
    blk_start = jnp.arange(n_blocks, dtype=I32) * MOE_BLOCK
    blk_eid = jnp.minimum(jnp.sum((pad_end[None, :] <= blk_start[:, None]).astype(I32), axis=1), N_EXPERTS - 1)
    n_used = (pad_end[-1:] // MOE_BLOCK).astype(I32)
    dest = pad_start[eid] + rank
    xin = _moe_dispatch(dest, h2, n_blocks * MOE_BLOCK)
    y = _moe_experts(blk_eid, n_used, xin, w13, w2)
    return _moe_combine(dest, jnp.swapaxes(wt, 0, 1), y, x1, gt_f, g_post)


def _layer(x, mods, shift_prev, rkv_last, s0, cache, p, w):
    b, t, d = x.shape
    ca = p["w0"].shape[-1]
    sh_a, sc_a, gt_a, sh_f, sc_f, gt_f = mods
    r, k, v, q, kb, vb = _in_proj(x, sc_a, sh_a, p["g_pre_mix"], w["w_main"], [F32, F32, F32, BF16, F32, F32],
                                  [1.0, 1.0, 1.0, Q_PRESCALE, 1.0, 1.0])
    wl, a, gate, logf, shift_new = _lora_heads(x, sc_a, sh_a, p["g_pre_mix"], shift_prev, p)
    ya, s_new = _rwkv7(r, k, v, wl, a, gate, rkv_last[..., :ca], rkv_last[..., ca:2 * ca], rkv_last[..., 2 * ca:],
                       s0, p)
    if cache is None:
        fq, _ = _forget_layouts(logf, None)
        yb = _fox_prompt(q, kb, vb, fq)
    else:
        k_past, v_past, logf_past = cache
        past = k_past.shape[1]
        fq, fk = _forget_layouts(logf, logf_past)
        yb = _fox_sample(q, kb, vb, k_past.astype(BF16).reshape(b, past, -1),
                         v_past.astype(BF16).reshape(b, past, -1), fq, fk)
    x1, h2, eid, wt = _postmix(ya, yb, x, gt_a, sc_f, sh_f, p["g_post_mix"], p["g_pre_ffn"],
                               w["wo_a"], w["wo_b"], w["wr_hi"], w["wr_lo"], w["br"])
    out = _hier_moe(h2, eid, wt, x1, gt_f, p["g_post_ffn"], w["w13"], w["w2"])
    nh = logf.shape[-1]
    return (out, shift_new.reshape(b, d), s_new,
            kb.reshape(b, t, nh, HEAD_DIM), vb.reshape(b, t, nh, HEAD_DIM), logf)


def kernel(x_prompt, x_sample, state_shift, state_wkv, cache_k, cache_v, cache_logf, c_prompt, c_sample, w_ada, b_ada, g_pre_mix, g_post_mix, g_pre_ffn, g_post_ffn, w_in, b_f, mu_rkv, mu_wag, w0, w_lora_a, w_lora_b, a0, a_lora_a, a_lora_b, g_lora_a, g_lora_b, k_k, k_a, r_k, ln_x_w, ln_x_b, w_out, w_rg, b_rg, w_re, b_re, w1, w3, w2):
    stacked = dict(w_ada=w_ada, b_ada=b_ada, g_pre_mix=g_pre_mix, g_post_mix=g_post_mix, g_pre_ffn=g_pre_ffn,
                   g_post_ffn=g_post_ffn, w_in=w_in, b_f=b_f, mu_rkv=mu_rkv, mu_wag=mu_wag, w0=w0,
                   w_lora_a=w_lora_a, w_lora_b=w_lora_b, a0=a0, a_lora_a=a_lora_a, a_lora_b=a_lora_b,
                   g_lora_a=g_lora_a, g_lora_b=g_lora_b, k_k=k_k, k_a=k_a, r_k=r_k, ln_x_w=ln_x_w,
                   ln_x_b=ln_x_b, w_out=w_out, w_rg=w_rg, b_rg=b_rg, w_re=w_re, b_re=b_re, w1=w1, w3=w3, w2=w2)
    depth = w_ada.shape[0]
    assert depth == 1, "one layer per call"
    p = {name: arr[0] for name, arr in stacked.items()}
    bp, _, d = x_prompt.shape
    bs = x_sample.shape[0]
    ca = p["w0"].shape[-1]
    n_main = 3 * ca + 3 * (d - ca)
    p["w_forget"] = p["w_in"][:, n_main:]

    wr = _pad_to(jnp.concatenate([p["w_re"], p["w_rg"]], axis=1).T, 0, ROUTER_ROWS)
    wr_hi = wr.astype(BF16)
    w = dict(
        w_main=p["w_in"][:, :n_main].astype(BF16),
        wo_a=p["w_out"][:ca].astype(BF16), wo_b=p["w_out"][ca:].astype(BF16),
        wr_hi=wr_hi, wr_lo=(wr - wr_hi.astype(F32)).astype(BF16),
        br=_pad_to(jnp.concatenate([p["b_re"], p["b_rg"]]), 0, ROUTER_ROWS).reshape(ROUTER_ROWS, 1),
        w13=jnp.concatenate([p["w1"], p["w3"]], axis=-1).astype(BF16), w2=p["w2"].astype(BF16),
    )

    mod = _ada_mod(jnp.concatenate([c_prompt, c_sample], axis=0), p["w_ada"], p["b_ada"])
    mods = [mod[:, None, i * d:(i + 1) * d] for i in range(6)]
    mods_p = [m[:bp] for m in mods]
    mods_s = [m[bp:] for m in mods]

    out_p = _layer(x_prompt, mods_p, jnp.zeros((bp, 1, d), F32), jnp.zeros((bp, 1, 3 * ca), F32),
                   jnp.zeros((bp,) + state_wkv.shape[2:], F32), None, p, w)
    rkv_last = _matmul_rows(state_shift[0], w["w_main"][:, :3 * ca], tn=ca)[:, None, :]
    out_s = _layer(x_sample, mods_s, state_shift[0][:, None, :], rkv_last, state_wkv[0],
                   (cache_k[0], cache_v[0], cache_logf[0]), p, w)
    yp, shp, wkvp, kp, vp, lfp = out_p
    ys, shs, wkvs, ks_, vs_, lfs = out_s
    return (yp, ys, shp[None], wkvp[None], kp[None], vp[None], lfp[None],
            shs[None], wkvs[None], ks_[None], vs_[None], lfs[None])
```

```python
import functools
import math

import jax
import jax.numpy as jnp
from jax import lax
from jax.experimental import pallas as pl
from jax.experimental.pallas import tpu as pltpu

F32 = jnp.float32
BF16 = jnp.bfloat16
I32 = jnp.int32

HEAD_DIM = 64
LANES = 128
PAIR = 2 * HEAD_DIM
RMS_EPS = 1e-6
GN_EPS = 64e-5
N_GROUPS = 4
EXPERTS_PER_GROUP = 8
N_EXPERTS = N_GROUPS * EXPERTS_PER_GROUP
TOP_K = 2
CHUNK = 64
VMEM_LIMIT_BYTES = 56 * 1024 * 1024
ROW_TILE = 512
MOE_BLOCK = 256


def _cparams(*sem):
    return pltpu.CompilerParams(dimension_semantics=sem, vmem_limit_bytes=VMEM_LIMIT_BYTES)


def _dot(a, b):
    return jnp.dot(a, b, preferred_element_type=F32)


def _dot_nt(a, b):
    return lax.dot_general(a, b, (((1,), (1,)), ((), ())), preferred_element_type=F32)


def _dot_tn(a, b):
    return lax.dot_general(a, b, (((0,), (0,)), ((), ())), preferred_element_type=F32)


def _split3(x):
    hi = x.astype(BF16)
    r1 = x - hi.astype(F32)
    mid = r1.astype(BF16)
    lo = (r1 - mid.astype(F32)).astype(BF16)
    return hi, mid, lo


def _split2(x):
    hi = x.astype(BF16)
    return hi, (x - hi.astype(F32)).astype(BF16)


def _dot3_exact_rhs(x, m):
    hi, mid, lo = _split3(x)
    return _dot(hi, m) + _dot(mid, m) + _dot(lo, m)


def _sigmoid(x):
    return 1.0 / (1.0 + jnp.exp(-x))


def _log_sigmoid(x):
    return jnp.minimum(x, 0.0) - jnp.log(1.0 + jnp.exp(-jnp.abs(x)))


def _mod_norm(x, g, scale, shift):
    y = x * lax.rsqrt(jnp.mean(x * x, axis=-1, keepdims=True) + RMS_EPS)
    return (y * g) * (1.0 + scale) + shift


def _row_tiling(batch, seq):
    if seq >= ROW_TILE:
        assert seq % ROW_TILE == 0
        return 1, ROW_TILE
    assert ROW_TILE % seq == 0 and batch % (ROW_TILE // seq) == 0
    return ROW_TILE // seq, seq


def _ada_kernel(c_ref, w_ref, b_ref, o_ref):
    c = c_ref[...]
    s = c * _sigmoid(c)
    o_ref[...] = _dot(s.astype(BF16), w_ref[...].astype(BF16)) + b_ref[...]


def _ada_mod(c_all, w_ada, b_ada):
    m, d = c_all.shape
    n = w_ada.shape[1]
    tn = 1024
    return pl.pallas_call(
        _ada_kernel,
        out_shape=jax.ShapeDtypeStruct((m, n), F32),
        grid=(n // tn,),
        in_specs=[pl.BlockSpec((m, d), lambda j: (0, 0)),
                  pl.BlockSpec((d, tn), lambda j: (0, j)),
                  pl.BlockSpec((1, tn), lambda j: (0, j))],
        out_specs=pl.BlockSpec((m, tn), lambda j: (0, j)),
        compiler_params=_cparams("arbitrary"),
        name="ada_mod",
    )(c_all, w_ada, b_ada.reshape(1, n))


def _proj_kernel(x_ref, sc_ref, sh_ref, g_ref, w_ref, *refs, out_scales):
    outs, hs_ref = refs[:-1], refs[-1]
    j = pl.program_id(2)
    bb, tt, d = x_ref.shape

    @pl.when(j == 0)
    def _():
        h = _mod_norm(x_ref[...], g_ref[...], sc_ref[...], sh_ref[...])
        hs_ref[...] = h.reshape(bb * tt, d).astype(BF16)

    acc = _dot(hs_ref[...], w_ref[...])
    for idx, o_ref in enumerate(outs):
        @pl.when(j == idx)
        def _(o_ref=o_ref, scale=out_scales[idx]):
            scaled = acc if scale == 1.0 else acc * scale
            o_ref[...] = scaled.reshape(o_ref.shape).astype(o_ref.dtype)


def _in_proj(x, sc, sh, g, w_main, out_dtypes, out_scales):
    b, t, d = x.shape
    n_out = len(out_dtypes)
    cw = w_main.shape[1] // n_out
    bb, tt = _row_tiling(b, t)
    row_spec = pl.BlockSpec((bb, tt, d), lambda ib, it, j: (ib, it, 0))
    mod_spec = pl.BlockSpec((bb, 1, d), lambda ib, it, j: (ib, 0, 0))
    out_spec = pl.BlockSpec((bb, tt, cw), lambda ib, it, j: (ib, it, 0))
    return pl.pallas_call(
        functools.partial(_proj_kernel, out_scales=tuple(out_scales)),
        out_shape=[jax.ShapeDtypeStruct((b, t, cw), dt) for dt in out_dtypes],
        grid=(b // bb, t // tt, n_out),
        in_specs=[row_spec, mod_spec, mod_spec,
                  pl.BlockSpec((1, 1, d), lambda ib, it, j: (0, 0, 0)),
                  pl.BlockSpec((d, cw), lambda ib, it, j: (0, j))],
        out_specs=[out_spec] * n_out,
        scratch_shapes=[pltpu.VMEM((bb * tt, d), BF16)],
        compiler_params=_cparams("arbitrary", "arbitrary", "arbitrary"),
        name="in_proj",
    )(x, sc, sh, g.reshape(1, 1, d), w_main)


def _lora_kernel(x_ref, sc_ref, sh_ref, g_ref, prev_ref, muw_ref, mua_ref, mug_ref,
                 aw_ref, bw_ref, w0_ref, aa_ref, ba_ref, a0_ref, ag_ref, bg_ref, wf_ref, bf_ref,
                 wl_ref, a_ref, gate_ref, lf_ref, shn_ref, carry_ref):
    it = pl.program_id(1)
    bb, tt, d = x_ref.shape
    rows = bb * tt
    h = _mod_norm(x_ref[...], g_ref[...], sc_ref[...], sh_ref[...])

    @pl.when(it == 0)
    def _():
        carry_ref[...] = prev_ref[...]

    rolled = pltpu.roll(h.reshape(rows, d), 1, 0).reshape(bb, tt, d)
    tpos = lax.broadcasted_iota(I32, (bb, tt, d), 1)
    h_prev = jnp.where(tpos == 0, carry_ref[...], rolled)
    last = h[:, tt - 1:tt, :]
    carry_ref[...] = last
    shn_ref[...] = last

    xx = h_prev - h

    def mix(mu_ref):
        return (h + xx * mu_ref[...]).reshape(rows, d).astype(BF16)

    ca = wl_ref.shape[-1]
    lw = jnp.tanh(_dot(mix(muw_ref), aw_ref[...]))
    w = w0_ref[...] + _dot(lw.astype(BF16), bw_ref[...])
    wl_ref[...] = (-math.exp(-0.5) * _sigmoid(w)).reshape(bb, tt, ca)
    la = _dot(mix(mua_ref), aa_ref[...])
    a_ref[...] = _sigmoid(a0_ref[...] + _dot(la.astype(BF16), ba_ref[...])).reshape(bb, tt, ca)
    lg = _sigmoid(_dot(mix(mug_ref), ag_ref[...]))
    gate_ref[...] = _dot(lg.astype(BF16), bg_ref[...]).reshape(bb, tt, ca).astype(gate_ref.dtype)
    nh = lf_ref.shape[-1]
    z = _dot(h.reshape(rows, d).astype(BF16), wf_ref[...]) + bf_ref[...]
    lf_ref[...] = _log_sigmoid(z)[:, :nh].reshape(bb, tt, nh)


def _pad_to(x, axis, size):
    pad = [(0, 0)] * x.ndim
    pad[axis] = (0, size - x.shape[axis])
    return jnp.pad(x, pad)


def _round_up(n, m):
    return (n + m - 1) // m * m


def _lora_heads(x, sc, sh, g, shift_prev, p):
    b, t, d = x.shape
    ca = p["w0"].shape[-1]
    nh = p["b_f"].shape[-1]
    bb, tt = _row_tiling(b, t)

    def lora_pair(a_w, b_w):
        r = _round_up(a_w.shape[1], LANES)
        return _pad_to(a_w, 1, r).astype(BF16), _pad_to(b_w, 0, r).astype(BF16)

    aw, bw = lora_pair(p["w_lora_a"], p["w_lora_b"])
    aa, ba = lora_pair(p["a_lora_a"], p["a_lora_b"])
    ag, bg = lora_pair(p["g_lora_a"], p["g_lora_b"])
    wf = _pad_to(p["w_forget"], 1, LANES).astype(BF16)
    bf = _pad_to(p["b_f"].reshape(1, nh), 1, LANES)
    mu = p["mu_wag"].reshape(3, 1, 1, d)

    row_spec = pl.BlockSpec((bb, tt, d), lambda ib, it: (ib, it, 0))
    mod_spec = pl.BlockSpec((bb, 1, d), lambda ib, it: (ib, 0, 0))
    vec_spec = pl.BlockSpec((1, 1, d), lambda ib, it: (0, 0, 0))

    def full(a):
        return pl.BlockSpec(a.shape, lambda ib, it: (0,) * a.ndim)

    ca_spec = pl.BlockSpec((bb, tt, ca), lambda ib, it: (ib, it, 0))
    weights = [aw, bw, p["w0"].reshape(1, ca), aa, ba, p["a0"].reshape(1, ca), ag, bg, wf, bf]
    return pl.pallas_call(
        _lora_kernel,
        out_shape=[jax.ShapeDtypeStruct((b, t, ca), F32), jax.ShapeDtypeStruct((b, t, ca), F32),
                   jax.ShapeDtypeStruct((b, t, ca), BF16), jax.ShapeDtypeStruct((b, t, nh), F32),
                   jax.ShapeDtypeStruct((b, 1, d), F32)],
        grid=(b // bb, t // tt),
        in_specs=[row_spec, mod_spec, mod_spec, vec_spec, mod_spec, vec_spec, vec_spec, vec_spec]
                 + [full(a) for a in weights],
        out_specs=[ca_spec, ca_spec, ca_spec,
                   pl.BlockSpec((bb, tt, nh), lambda ib, it: (ib, it, 0)),
                   pl.BlockSpec((bb, 1, d), lambda ib, it: (ib, 0, 0))],
        scratch_shapes=[pltpu.VMEM((bb, 1, d), F32)],
        compiler_params=_cparams("arbitrary", "arbitrary"),
        name="lora_heads",
    )(x, sc, sh, g.reshape(1, 1, d), shift_prev, mu[0], mu[1], mu[2], *weights)


def _cumsum_kernel(x_ref, o_ref, carry_ref):
    it = pl.program_id(1)
    bb, nh, tl = x_ref.shape

    @pl.when(it == 0)
    def _():
        carry_ref[...] = jnp.zeros_like(carry_ref)

    r = lax.broadcasted_iota(I32, (tl, tl), 0)
    c = lax.broadcasted_iota(I32, (tl, tl), 1)
    upper = jnp.where(r <= c, 1.0, 0.0).astype(BF16)
    cs = _dot3_exact_rhs(x_ref[...].reshape(bb * nh, tl), upper) + carry_ref[...]
    o_ref[...] = cs.reshape(bb, nh, tl)
    carry_ref[...] = cs[:, tl - 1:tl]


def _cumsum_lanes(x):
    b, nh, l = x.shape
    tl = l if l <= 2304 else 1024
    assert l % tl == 0 and nh % 8 == 0
    bb = math.gcd(b, 8)
    return pl.pallas_call(
        _cumsum_kernel,
        out_shape=jax.ShapeDtypeStruct((b, nh, l), F32),
        grid=(b // bb, l // tl),
        in_specs=[pl.BlockSpec((bb, nh, tl), lambda ib, it: (ib, 0, it))],
        out_specs=pl.BlockSpec((bb, nh, tl), lambda ib, it: (ib, 0, it)),
        scratch_shapes=[pltpu.VMEM((bb * nh, 1), F32)],
        compiler_params=_cparams("arbitrary", "arbitrary"),
        name="logf_cumsum",
    )(x)


NEG_BIG = -1e30


def _head_masked(q):
    lane = lax.broadcasted_iota(I32, q.shape, 1)
    first = lane < HEAD_DIM
    zero = jnp.zeros_like(q)
    return first, (jnp.where(first, q, zero), jnp.where(first, zero, q))


FOX_TILE = 512
LOG2E = 1.0 / math.log(2.0)
Q_PRESCALE = HEAD_DIM ** -0.5 * LOG2E


def _with_bias_lanes(x, first, f_col, own_sign):
    lane = lax.broadcasted_iota(I32, x.shape, 1)
    out = []
    for e in range(2):
        keep = first if e == 0 else jnp.logical_not(first)
        base = HEAD_DIM if e == 0 else 0
        terms = [t.astype(F32) * own_sign for t in _split3(f_col[:, e:e + 1])]
        ones = jnp.ones_like(terms[0])
        cols = terms + [ones] * 3 if own_sign < 0 else [ones] * 3 + terms
        y = jnp.where(keep, x, 0.0)
        for i, col in enumerate(cols):
            y = jnp.where(lane == base + i, col, y)
        out.append(y.astype(BF16))
    return out


def _fox_prompt_kernel(q_ref, k_ref, v_ref, f_ref, o_ref, k0_ref, k1_ref, vt_ref):
    t = k_ref.shape[0]
    tq = tk = FOX_TILE
    half = HEAD_DIM
    first_t = lax.broadcasted_iota(I32, (t, PAIR), 1) < half
    kaug = _with_bias_lanes(k_ref[...], first_t, f_ref[...], -1.0)
    k0_ref[...] = kaug[0]
    k1_ref[...] = kaug[1]
    kaug_refs = (k0_ref, k1_ref)
    for i in range(t // tk):
        vt_ref[:, i * tk:(i + 1) * tk] = v_ref[i * tk:(i + 1) * tk, :].T.astype(BF16)

    first_q = lax.broadcasted_iota(I32, (tq, PAIR), 1) < half
    krow = lax.broadcasted_iota(I32, (tk, tq), 0)
    qcol = lax.broadcasted_iota(I32, (tk, tq), 1)

    def q_block(qi, carry):
        q0 = pl.multiple_of(qi * tq, tq)
        qaug = _with_bias_lanes(q_ref[pl.ds(q0, tq), :].astype(F32), first_q, f_ref[pl.ds(q0, tq), :], 1.0)

        def kv_blocks(state, blocks):
            starts = [pl.multiple_of(k0, tk) for k0, _ in blocks]
            s = [[_dot_nt(kaug_refs[e][pl.ds(k0, tk), :], qaug[e]) for e in range(2)] for k0 in starts]
            state = list(state)
            for (_, masked), k0, s_blk in zip(blocks, starts, s):
                for e in range(2):
                    m_prev, l_prev, acc = state[e]
                    s_e = jnp.where(krow <= qcol, s_blk[e], NEG_BIG) if masked else s_blk[e]
                    m_new = jnp.maximum(m_prev, jnp.max(s_e, axis=0, keepdims=True))
                    p = jnp.exp2(s_e - m_new)
                    alpha = jnp.exp2(m_prev - m_new)
                    l_new = alpha * l_prev + jnp.sum(p, axis=0, keepdims=True)
                    pv = _dot(vt_ref[e * half:(e + 1) * half, pl.ds(k0, tk)], p.astype(BF16))
                    state[e] = (m_new, l_new, acc * alpha + pv)
            return tuple(state)

        neg = jnp.full((1, tq), NEG_BIG, F32)
        zero = jnp.zeros((1, tq), F32)
        init = tuple((neg, zero, jnp.zeros((half, tq), F32)) for _ in range(2))
        state = lax.fori_loop(
            0, qi // 2, lambda i, st: kv_blocks(st, [(2 * i * tk, False), ((2 * i + 1) * tk, False)]), init)
        state = lax.cond(
            qi % 2 == 1,
            lambda st: kv_blocks(st, [((qi - 1) * tk, False), (qi * tk, True)]),
            lambda st: kv_blocks(st, [(qi * tk, True)]),
            state)
        o_t = jnp.concatenate([state[e][2] / state[e][1] for e in range(2)], axis=0)
        o_ref[pl.ds(q0, tq), :] = o_t.T.astype(o_ref.dtype)
        return carry

    lax.fori_loop(0, t // tq, q_block, 0)


def _fox_prompt(q, k, v, f_rows):
    b, t, cb = q.shape
    hp = cb // PAIR
    assert t % FOX_TILE == 0
    seq_spec = pl.BlockSpec((None, t, PAIR), lambda ib, ih: (ib, 0, ih))
    return pl.pallas_call(
        _fox_prompt_kernel,
        out_shape=jax.ShapeDtypeStruct((b, t, cb), BF16),
        grid=(b, hp),
        in_specs=[seq_spec, seq_spec, seq_spec,
                  pl.BlockSpec((None, None, t, 2), lambda ib, ih: (ib, ih, 0, 0))],
        out_specs=seq_spec,
        scratch_shapes=[pltpu.VMEM((t, PAIR), BF16), pltpu.VMEM((t, PAIR), BF16), pltpu.VMEM((PAIR, t), BF16)],
        compiler_params=_cparams("arbitrary", "arbitrary"),
        name="fox_prompt",
    )(q, k, v, f_rows)


CACHE_ROWS = 512


def _fox_sample_kernel(q_ref, kn_ref, vn_ref, kc_ref, vc_ref, fq_ref, fk_ref, o_ref, kf_ref, vf_ref):
    j = pl.program_id(1)
    ts = q_ref.shape[0]
    past = kf_ref.shape[0]
    pr, nh, n = kc_ref.shape
    n_pairs = q_ref.shape[1] // PAIR
    rows = pl.ds(pl.multiple_of(j * pr, pr), pr)
    kf_ref[rows, :] = kc_ref[...].reshape(pr, nh * n).astype(BF16)
    vf_ref[rows, :] = vc_ref[...].reshape(pr, nh * n).astype(BF16)

    @pl.when(j == pl.num_programs(1) - 1)
    def _():
        row = lax.broadcasted_iota(I32, (ts, ts), 0)
        col = lax.broadcasted_iota(I32, (ts, ts), 1)
        for pp in range(n_pairs):
            lanes = slice(pp * PAIR, (pp + 1) * PAIR)
            first, qs = _head_masked(q_ref[:, lanes])
            kc = kf_ref[:, lanes]
            vc = vf_ref[:, lanes]
            kn = kn_ref[:, lanes].astype(BF16)
            vn = vn_ref[:, lanes].astype(BF16)
            outs = []
            for e in range(2):
                fq = fq_ref[pp, :, e:e + 1]
                s_p = _dot_nt(qs[e], kc) + (fq - fk_ref[pp, e:e + 1, 0:past])
                s_n = _dot_nt(qs[e], kn) + (fq - fk_ref[pp, e:e + 1, past:past + ts])
                s_n = jnp.where(col <= row, s_n, NEG_BIG)
                m = jnp.maximum(jnp.max(s_p, axis=1, keepdims=True), jnp.max(s_n, axis=1, keepdims=True))
                p_p = jnp.exp2(s_p - m)
                p_n = jnp.exp2(s_n - m)
                l = jnp.sum(p_p, axis=1, keepdims=True) + jnp.sum(p_n, axis=1, keepdims=True)
                outs.append((_dot(p_p.astype(BF16), vc) + _dot(p_n.astype(BF16), vn)) / l)
            o_ref[:, lanes] = jnp.where(first, outs[0], outs[1]).astype(o_ref.dtype)


def _fox_sample(q, kn, vn, kc, vc, fq, fk):
    b, ts, cb = q.shape
    _, past, nh, n = kc.shape
    lp = fk.shape[-1]
    hp = cb // PAIR
    pr = min(past, CACHE_ROWS)
    assert past % pr == 0
    new_spec = pl.BlockSpec((None, ts, cb), lambda ib, j: (ib, 0, 0))
    past_spec = pl.BlockSpec((None, pr, nh, n), lambda ib, j: (ib, j, 0, 0))
    return pl.pallas_call(
        _fox_sample_kernel,
        out_shape=jax.ShapeDtypeStruct((b, ts, cb), BF16),
        grid=(b, past // pr),
        in_specs=[new_spec, new_spec, new_spec, past_spec, past_spec,
                  pl.BlockSpec((None, hp, ts, 2), lambda ib, j: (ib, 0, 0, 0)),
                  pl.BlockSpec((None, hp, 2, lp), lambda ib, j: (ib, 0, 0, 0))],
        out_specs=new_spec,
        scratch_shapes=[pltpu.VMEM((past, cb), BF16), pltpu.VMEM((past, cb), BF16)],
        compiler_params=_cparams("arbitrary", "arbitrary"),
        name="fox_sample",
    )(q, kn, vn, kc, vc, fq, fk)


def _forget_layouts(logf_new, logf_past):
    b, t, nh = logf_new.shape
    lt = jnp.swapaxes(logf_new, 1, 2)
    past = 0
    if logf_past is not None:
        past = logf_past.shape[1]
        lt = jnp.concatenate([jnp.swapaxes(logf_past, 1, 2), lt], axis=2)
    lp = _round_up(past + t, LANES)
    ft = _cumsum_lanes(_pad_to(lt, 2, lp)) * LOG2E
    fk = ft.reshape(b, nh // 2, 2, lp)
    fq = jnp.swapaxes(fk[..., past:past + t], 2, 3)
    return fq, fk


def _rwkv_kernel(r_ref, k_ref, v_ref, wl_ref, a_ref, g_ref, rl_ref, kl_ref, vl_ref, s0_ref,
                 mur_ref, muk_ref, muv_ref, kk_ref, ka_ref, rk_ref, lnw_ref, lnb_ref,
                 y_ref, sout_ref, st_ref, prev_ref):
    itb = pl.program_id(2)
    tb, lanes = r_ref.shape
    n_pairs = lanes // PAIR
    c = min(CHUNK, tb)

    pairs = range(n_pairs)
    heads = range(2)

    @pl.when(itb == 0)
    def _():
        zero = jnp.zeros((HEAD_DIM, HEAD_DIM), F32)
        for p in pairs:
            st_ref[p] = jnp.concatenate([jnp.concatenate([s0_ref[2 * p], zero], axis=1),
                                         jnp.concatenate([zero, s0_ref[2 * p + 1]], axis=1)], axis=0)
        prev_ref[0:1, :] = rl_ref[...]
        prev_ref[1:2, :] = kl_ref[...]
        prev_ref[2:3, :] = vl_ref[...]
    ri = lax.broadcasted_iota(I32, (c, 2 * c), 0)
    ci = lax.broadcasted_iota(I32, (c, 2 * c), 1) % c
    strict2 = ri > ci
    incl2 = ri >= ci
    tri_ones = jnp.where(incl2[:, :c], 1.0, 0.0).astype(BF16)
    li = lax.broadcasted_iota(I32, (PAIR, PAIR), 0) // HEAD_DIM
    lj = lax.broadcasted_iota(I32, (PAIR, PAIR), 1) // HEAD_DIM
    same_head = li == lj
    head_ones = jnp.where(same_head, 1.0, 0.0).astype(BF16)
    first = lax.broadcasted_iota(I32, (c, PAIR), 1) < HEAD_DIM
    row0 = lax.broadcasted_iota(I32, (c, PAIR), 0) == 0
    zeros_cb = jnp.zeros((c, PAIR), BF16)

    def head_sum2(x):
        hi, lo = _split2(x)
        s = _dot(jnp.concatenate([hi, lo], axis=0), head_ones)
        return s[:c] + s[c:]

    def shifted(x, prev_row):
        return jnp.where(row0, prev_row, pltpu.roll(x, 1, 0))

    def by_head(x):
        return [jnp.where(first, x, zeros_cb), jnp.where(first, zeros_cb, x)]

    def chunk(ic, carry):
        rows = pl.ds(pl.multiple_of(ic * c, c), c)
        lns = [slice(pp * PAIR, (pp + 1) * PAIR) for pp in pairs]
        r, k, v, wl, a, kk = [], [], [], [], [], []
        for ln in lns:
            r_raw, k_raw, v_raw = r_ref[rows, ln], k_ref[rows, ln], v_ref[rows, ln]
            r.append(r_raw + mur_ref[:, ln] * (shifted(r_raw, prev_ref[0:1, ln]) - r_raw))
            k.append(k_raw + muk_ref[:, ln] * (shifted(k_raw, prev_ref[1:2, ln]) - k_raw))
            v.append(v_raw + muv_ref[:, ln] * (shifted(v_raw, prev_ref[2:3, ln]) - v_raw))
            prev_ref[0:1, ln] = r_raw[c - 1:c, :]
            prev_ref[1:2, ln] = k_raw[c - 1:c, :]
            prev_ref[2:3, ln] = v_raw[c - 1:c, :]
            wl.append(wl_ref[rows, ln])
            a.append(a_ref[rows, ln])
            kk.append(k[-1] * kk_ref[:, ln])
        lcum = []
        for p in pairs:
            s = _dot(tri_ones, jnp.concatenate(_split3(wl[p]), axis=1))
            lcum.append(s[:, :PAIR] + s[:, PAIR:2 * PAIR] + s[:, 2 * PAIR:])
        kk_ss = [head_sum2(x * x) for x in kk]
        kf = [k[p] * (1.0 + (a[p] - 1.0) * ka_ref[:, lns[p]]) for p in pairs]
        bonus_s = [_dot((r[p] * kf[p] * rk_ref[:, lns[p]]).astype(BF16), head_ones) for p in pairs]
        kk = [kk[p] / jnp.maximum(jnp.sqrt(kk_ss[p]), 1e-12) for p in pairs]
        b = [kk[p] * a[p] for p in pairs]
        inv = [jnp.exp(-lcum[p]) for p in pairs]
        ltot = [lcum[p][c - 1:c, :] for p in pairs]
        suffix = [jnp.exp(ltot[p] - lcum[p]) for p in pairs]
        r_bar = [(r[p] * jnp.exp(lcum[p])).astype(BF16) for p in pairs]
        a_bar = [(-kk[p] * jnp.exp(lcum[p] - wl[p])).astype(BF16) for p in pairs]
        bk_bar = [jnp.concatenate([(b[p] * inv[p]).astype(BF16), (kf[p] * inv[p]).astype(BF16)], axis=0)
                  for p in pairs]
        bk_suf = [jnp.concatenate([(b[p] * suffix[p]).astype(BF16), (kf[p] * suffix[p]).astype(BF16)], axis=0)
                  for p in pairs]
        vb = [x.astype(BF16) for x in v]
        st = [st_ref[p] for p in pairs]
        stb = [x.astype(BF16) for x in st]

        prod = [_dot_nt(jnp.concatenate(by_head(a_bar[p]) + by_head(r_bar[p]), axis=0), bk_bar[p]) for p in pairs]
        a_full = [[jnp.where(strict2, prod[p][e * c:(e + 1) * c], 0.0).astype(BF16) for e in heads] for p in pairs]
        r_full = [[jnp.where(incl2, prod[p][(2 + e) * c:(3 + e) * c], 0.0).astype(BF16) for e in heads]
                  for p in pairs]
        on_s = [_dot_nt(jnp.concatenate([a_bar[p], r_bar[p]], axis=0), stb[p]) for p in pairs]
        zv = [jnp.concatenate([zeros_cb, vb[p]], axis=0) for p in pairs]
        rhs_v = [[_dot(a_full[p][e], zv[p]) for e in heads] for p in pairs]
        x = [[on_s[p][:c] + jnp.where(first, rhs_v[p][0], rhs_v[p][1])] * 2 for p in pairs]
        x = [list(xp) for xp in x]
        ap = [[a_full[p][e][:, :c] for e in heads] for p in pairs]
        span = 1
        while True:
            span *= 2
            if span >= c:
                x = [[x[p][e] + _dot(ap[p][e], x[p][e].astype(BF16)) for e in heads] for p in pairs]
                break
            res = [[_dot(ap[p][e], jnp.concatenate([x[p][e].astype(BF16), ap[p][e]], axis=1)) for e in heads]
                   for p in pairs]
            x = [[x[p][e] + res[p][e][:, :PAIR] for e in heads] for p in pairs]
            ap = [[res[p][e][:, PAIR:].astype(BF16) for e in heads] for p in pairs]
        uv = [jnp.concatenate([jnp.where(first, x[p][0], x[p][1]).astype(BF16), vb[p]], axis=0) for p in pairs]

        y_h = [[_dot(r_full[p][e], uv[p]) for e in heads] for p in pairs]
        upd = [_dot_tn(uv[p], bk_suf[p]) for p in pairs]
        for p in pairs:
            st_ref[p] = st[p] * jnp.exp(ltot[p]) + jnp.where(same_head, upd[p], 0.0)

        y = [on_s[p][c:] + jnp.where(first, y_h[p][0], y_h[p][1]) for p in pairs]
        mean = [head_sum2(y[p]) * (1.0 / HEAD_DIM) for p in pairs]
        dy = [y[p] - mean[p] for p in pairs]
        var = [head_sum2(dy[p] * dy[p]) * (1.0 / HEAD_DIM) for p in pairs]
        for p in pairs:
            ln = lns[p]
            yn = dy[p] * lax.rsqrt(var[p] + GN_EPS) * lnw_ref[:, ln] + lnb_ref[:, ln]
            y_ref[rows, ln] = ((yn + bonus_s[p] * v[p]) * g_ref[rows, ln].astype(F32)).astype(y_ref.dtype)
        return carry

    lax.fori_loop(0, tb // c, chunk, 0)

    @pl.when(itb == pl.num_programs(2) - 1)
    def _():
        for p in pairs:
            st = st_ref[p]
            sout_ref[2 * p] = st[:HEAD_DIM, :HEAD_DIM]
            sout_ref[2 * p + 1] = st[HEAD_DIM:, HEAD_DIM:]


def _rwkv7(r, k, v, wl, a, g, r_last, k_last, v_last, s0, p, pairs_per_step=8):
    b, t, ca = r.shape
    n_pairs = ca // PAIR
    pps = min(pairs_per_step, n_pairs)
    lanes = pps * PAIR
    tb = min(t, 256)
    assert t % tb == 0 and tb % min(CHUNK, tb) == 0
    seq_spec = pl.BlockSpec((None, tb, lanes), lambda ib, ig, it: (ib, it, ig))
    row_spec = pl.BlockSpec((None, 1, lanes), lambda ib, ig, it: (ib, 0, ig))
    st_spec = pl.BlockSpec((None, 2 * pps, HEAD_DIM, HEAD_DIM), lambda ib, ig, it: (ib, ig, 0, 0))
    par_spec = pl.BlockSpec((1, lanes), lambda ib, ig, it: (0, ig))
    mu = p["mu_rkv"].reshape(3, 1, ca)
    vecs = [mu[0], mu[1], mu[2], p["k_k"].reshape(1, ca), p["k_a"].reshape(1, ca), p["r_k"].reshape(1, ca),
            p["ln_x_w"].reshape(1, ca), p["ln_x_b"].reshape(1, ca)]
    return pl.pallas_call(
        _rwkv_kernel,
        out_shape=[jax.ShapeDtypeStruct((b, t, ca), BF16), jax.ShapeDtypeStruct(s0.shape, F32)],
        grid=(b, n_pairs // pps, t // tb),
        in_specs=[seq_spec] * 6 + [row_spec] * 3 + [st_spec] + [par_spec] * 8,
        out_specs=[seq_spec, st_spec],
        scratch_shapes=[pltpu.VMEM((pps, PAIR, PAIR), F32), pltpu.VMEM((8, lanes), F32)],
        compiler_params=_cparams("arbitrary", "arbitrary", "arbitrary"),
        name="rwkv7",
    )(r, k, v, wl, a, g, r_last, k_last, v_last, s0, *vecs)


def _matmul_kernel(a_ref, w_ref, o_ref):
    o_ref[...] = _dot(a_ref[...].astype(BF16), w_ref[...])


def _matmul_rows(a, w_bf16, tn=1024):
    m, kd = a.shape
    n = w_bf16.shape[1]
    assert n % tn == 0
    return pl.pallas_call(
        _matmul_kernel,
        out_shape=jax.ShapeDtypeStruct((m, n), F32),
        grid=(n // tn,),
        in_specs=[pl.BlockSpec((m, kd), lambda j: (0, 0)), pl.BlockSpec((kd, tn), lambda j: (0, j))],
        out_specs=pl.BlockSpec((m, tn), lambda j: (0, j)),
        compiler_params=_cparams("arbitrary"),
        name="matmul_rows",
    )(a, w_bf16)


ROUTER_ROWS = 40


def _rms(x, g):
    return x * lax.rsqrt(jnp.mean(x * x, axis=-1, keepdims=True) + RMS_EPS) * g


U32 = jnp.uint32
HIGH_HALF = 0xFFFF0000


def _pack_halves(x):
    n = x.shape[1] // 2
    lo = lax.bitcast_convert_type(x[:, :n].astype(BF16).astype(F32), U32)
    hi = lax.bitcast_convert_type(x[:, n:].astype(BF16).astype(F32), U32)
    return (hi & U32(HIGH_HALF)) | (lo >> U32(16))


def _unpack_halves(w):
    lo = lax.bitcast_convert_type(w << U32(16), F32)
    hi = lax.bitcast_convert_type(w & U32(HIGH_HALF), F32)
    return lo, hi


def _postmix_kernel(ya_ref, yb_ref, x_ref, gt_ref, sc_ref, sh_ref, gpost_ref, gpre_ref,
                    woa_ref, wob_ref, wrh_ref, wrl_ref, br_ref,
                    x1_ref, h2_ref, eid_ref, wt_ref):
    bb, tt, d = x_ref.shape
    rows = bb * tt
    o = (_dot(ya_ref[...].reshape(rows, ya_ref.shape[-1]), woa_ref[...])
         + _dot(yb_ref[...].reshape(rows, yb_ref.shape[-1]), wob_ref[...])).reshape(bb, tt, d)
    x1 = x_ref[...] + gt_ref[...] * _rms(o, gpost_ref[...])
    x1_ref[...] = x1
    h2 = _mod_norm(x1, gpre_ref[...], sc_ref[...], sh_ref[...]).reshape(rows, d)
    h2_ref[...] = _pack_halves(h2)

    hi = h2.astype(BF16)
    lo = (h2 - hi.astype(F32)).astype(BF16)
    logits = (_dot_nt(wrh_ref[...], hi) + _dot_nt(wrh_ref[...], lo) + _dot_nt(wrl_ref[...], hi)) + br_ref[...]
    le = logits[0:N_EXPERTS, :]
    lg = logits[N_EXPERTS:N_EXPERTS + N_GROUPS, :]
    gio = lax.broadcasted_iota(I32, lg.shape, 0)
    gmax = jnp.max(lg, axis=0, keepdims=True)
    gi = jnp.min(jnp.where(lg == gmax, gio, N_GROUPS), axis=0, keepdims=True)
    pg = 1.0 / jnp.sum(jnp.exp(lg - gmax), axis=0, keepdims=True)
    eio = lax.broadcasted_iota(I32, le.shape, 0)
    le1 = jnp.where(eio // EXPERTS_PER_GROUP == gi, le, NEG_BIG)
    m1 = jnp.max(le1, axis=0, keepdims=True)
    i1 = jnp.min(jnp.where(le1 == m1, eio, N_EXPERTS), axis=0, keepdims=True)
    le2 = jnp.where(eio == i1, NEG_BIG, le1)
    m2 = jnp.max(le2, axis=0, keepdims=True)
    i2 = jnp.min(jnp.where(le2 == m2, eio, N_EXPERTS), axis=0, keepdims=True)
    e2 = jnp.exp(m2 - m1)
    den = 1.0 + e2
    eid_ref[0:1, :] = i1
    eid_ref[1:2, :] = i2
    wt_ref[0:1, :] = pg / den
    wt_ref[1:2, :] = pg * e2 / den


def _postmix(ya, yb, x, gt, sc, sh, g_post, g_pre, wo_a, wo_b, wr_hi, wr_lo, br):
    b, t, d = x.shape
    bb, tt = _row_tiling(b, t)
    rows = bb * tt
    nt = t // tt
    n = b * t

    def row_spec(w):
        return pl.BlockSpec((bb, tt, w), lambda ib, it: (ib, it, 0))

    mod_spec = pl.BlockSpec((bb, 1, d), lambda ib, it: (ib, 0, 0))
    vec_spec = pl.BlockSpec((1, 1, d), lambda ib, it: (0, 0, 0))

    def full(a):
        return pl.BlockSpec(a.shape, lambda ib, it: (0,) * a.ndim)

    tok_spec = pl.BlockSpec((TOP_K, rows), lambda ib, it: (0, ib * nt + it))
    weights = [wo_a, wo_b, wr_hi, wr_lo, br]
    return pl.pallas_call(
        _postmix_kernel,
        out_shape=[jax.ShapeDtypeStruct((b, t, d), F32), jax.ShapeDtypeStruct((n, d // 2), U32),
                   jax.ShapeDtypeStruct((TOP_K, n), I32), jax.ShapeDtypeStruct((TOP_K, n), F32)],
        grid=(b // bb, nt),
        in_specs=[row_spec(ya.shape[-1]), row_spec(yb.shape[-1]), row_spec(d), mod_spec, mod_spec, mod_spec,
                  vec_spec, vec_spec] + [full(a) for a in weights],
        out_specs=[row_spec(d), pl.BlockSpec((rows, d // 2), lambda ib, it: (ib * nt + it, 0)), tok_spec, tok_spec],
        compiler_params=_cparams("arbitrary", "arbitrary"),
        name="postmix",
    )(ya, yb, x, gt, sc, sh, g_post.reshape(1, 1, d), g_pre.reshape(1, 1, d), *weights)


def _rank_kernel(eid_ref, rank_ref, cnt_ref, carry_ref):
    i = pl.program_id(0)
    tr = eid_ref.shape[1]

    @pl.when(i == 0)
    def _():
        carry_ref[...] = jnp.zeros_like(carry_ref)

    r = lax.broadcasted_iota(I32, (tr, tr), 0)
    c = lax.broadcasted_iota(I32, (tr, tr), 1)
    upper = jnp.where(r <= c, 1.0, 0.0).astype(BF16)
    eio = lax.broadcasted_iota(I32, (N_EXPERTS, tr), 0)
    base = carry_ref[...]
    for k in range(TOP_K):
        onehot = jnp.where(eio == eid_ref[k:k + 1, :], 1.0, 0.0)
        cum = _dot(onehot.astype(BF16), upper)
        rank = jnp.sum(onehot * (base + cum - onehot), axis=0, keepdims=True)
        rank_ref[k:k + 1, :] = rank.astype(I32)
        base = base + cum[:, tr - 1:tr]
    carry_ref[...] = base
    cnt_ref[...] = jnp.broadcast_to(base, cnt_ref.shape)


def _moe_rank(eid):
    n = eid.shape[1]
    tr = min(n, 512)
    assert n % tr == 0
    return pl.pallas_call(
        _rank_kernel,
        out_shape=[jax.ShapeDtypeStruct((TOP_K, n), I32), jax.ShapeDtypeStruct((N_EXPERTS, LANES), F32)],
        grid=(n // tr,),
        in_specs=[pl.BlockSpec((TOP_K, tr), lambda i: (0, i))],
        out_specs=[pl.BlockSpec((TOP_K, tr), lambda i: (0, i)),
                   pl.BlockSpec((N_EXPERTS, LANES), lambda i: (0, 0))],
        scratch_shapes=[pltpu.VMEM((N_EXPERTS, 1), F32)],
        compiler_params=_cparams("arbitrary"),
        name="moe_rank",
    )(eid)


def _dispatch_kernel(dest_ref, h_ref, xin0_ref, xin_ref, sem):
    del xin0_ref
    tt = h_ref.shape[0]

    def row_copy(t, dest):
        return pltpu.make_async_copy(h_ref.at[pl.ds(t, 1)], xin_ref.at[pl.ds(dest, 1)], sem)

    def issue(t, carry):
        for k in range(TOP_K):
            row_copy(t, dest_ref[k, t]).start()
        return carry

    def drain(t, carry):
        for k in range(TOP_K):
            row_copy(0, 0).wait()
        return carry

    lax.fori_loop(0, tt, issue, 0, unroll=8)
    lax.fori_loop(0, tt, drain, 0, unroll=8)


def _moe_dispatch(dest, h2, n_rows):
    n, d = h2.shape
    tt = min(n, MOE_BLOCK)
    return pl.pallas_call(
        _dispatch_kernel,
        out_shape=jax.ShapeDtypeStruct((n_rows, d), h2.dtype),
        grid=(n // tt,),
        in_specs=[pl.BlockSpec((TOP_K, tt), lambda i: (0, i), memory_space=pltpu.SMEM),
                  pl.BlockSpec((tt, d), lambda i: (i, 0)),
                  pl.BlockSpec(memory_space=pl.ANY)],
        out_specs=pl.BlockSpec(memory_space=pl.ANY),
        scratch_shapes=[pltpu.SemaphoreType.DMA(())],
        input_output_aliases={2: 0},
        compiler_params=_cparams("arbitrary"),
        name="moe_dispatch",
    )(dest, h2, jnp.zeros((n_rows, d), h2.dtype))


def _expert_kernel(be_ref, nu_ref, x_ref, w1_ref, w3_ref, w2_ref, y_ref, w1b_ref, w3b_ref, w2b_ref):
    i = pl.program_id(0)

    @pl.when(jnp.logical_or(i == 0, be_ref[i] != be_ref[jnp.maximum(i - 1, 0)]))
    def _():
        w1b_ref[...] = w1_ref[...].astype(BF16)
        w3b_ref[...] = w3_ref[...].astype(BF16)
        w2b_ref[...] = w2_ref[...].astype(BF16)

    @pl.when(i < nu_ref[0])
    def _():
        x_lo, x_hi = _unpack_halves(x_ref[...])
        n = x_lo.shape[1]
        x_lo, x_hi = x_lo.astype(BF16), x_hi.astype(BF16)
        h1 = _dot(x_lo, w1b_ref[0:n, :]) + _dot(x_hi, w1b_ref[n:2 * n, :])
        h3 = _dot(x_lo, w3b_ref[0:n, :]) + _dot(x_hi, w3b_ref[n:2 * n, :])
        hb = (h1 * _sigmoid(h1)) * h3
        y_ref[...] = _pack_halves(_dot(hb.astype(BF16), w2b_ref[...]))

    @pl.when(i >= nu_ref[0])
    def _():
        y_ref[...] = jnp.zeros_like(y_ref)


def _moe_experts(blk_eid, n_used, xin, w1, w3, w2):
    n_rows, dh = xin.shape
    e, f, d = w2.shape
    up_spec = pl.BlockSpec((None, d, f), lambda i, be, nu: (be[i], 0, 0))
    grid_spec = pltpu.PrefetchScalarGridSpec(
        num_scalar_prefetch=2,
        grid=(n_rows // MOE_BLOCK,),
        in_specs=[pl.BlockSpec((MOE_BLOCK, dh), lambda i, be, nu: (i, 0)), up_spec, up_spec,
                  pl.BlockSpec((None, f, d), lambda i, be, nu: (be[i], 0, 0))],
        out_specs=pl.BlockSpec((MOE_BLOCK, dh), lambda i, be, nu: (i, 0)),
        scratch_shapes=[pltpu.VMEM((d, f), BF16), pltpu.VMEM((d, f), BF16), pltpu.VMEM((f, d), BF16)],
    )
    return pl.pallas_call(
        _expert_kernel,
        out_shape=jax.ShapeDtypeStruct((n_rows, dh), U32),
        grid_spec=grid_spec,
        compiler_params=_cparams("arbitrary"),
        name="moe_experts",
    )(blk_eid, n_used, xin, w1, w3, w2)


def _combine_kernel(dest_ref, wt_ref, y_ref, x1_ref, gt_ref, g_ref, o_ref, buf_ref, sem):
    bb, tt, d = x1_ref.shape
    rows = bb * tt

    def row_copy(k, t, src):
        return pltpu.make_async_copy(y_ref.at[pl.ds(src, 1)], buf_ref.at[k, pl.ds(t, 1)], sem)

    def issue(t, carry):
        for k in range(TOP_K):
            row_copy(k, t, dest_ref[k, t]).start()
        return carry

    def drain(t, carry):
        for k in range(TOP_K):
            row_copy(k, 0, 0).wait()
        return carry

    lax.fori_loop(0, rows, issue, 0, unroll=8)
    lax.fori_loop(0, rows, drain, 0, unroll=8)
    wt = wt_ref[...]
    lo0, hi0 = _unpack_halves(buf_ref[0])
    lo1, hi1 = _unpack_halves(buf_ref[1])
    w0, w1 = wt[:, 0:1], wt[:, 1:2]
    f = jnp.concatenate([lo0 * w0 + lo1 * w1, hi0 * w0 + hi1 * w1], axis=1).reshape(bb, tt, d)
    o_ref[...] = x1_ref[...] + gt_ref[...] * _rms(f, g_ref[...])


def _moe_combine(dest, wt_rows, y, x1, gt, g_post):
    b, t, d = x1.shape
    if t >= MOE_BLOCK:
        bb, tt = 1, MOE_BLOCK
    else:
        bb, tt = MOE_BLOCK // t, t
    assert t % tt == 0 and b % bb == 0
    rows = bb * tt
    nt = t // tt
    return pl.pallas_call(
        _combine_kernel,
        out_shape=jax.ShapeDtypeStruct((b, t, d), F32),
        grid=(b // bb, nt),
        in_specs=[pl.BlockSpec((TOP_K, rows), lambda ib, it: (0, ib * nt + it), memory_space=pltpu.SMEM),
                  pl.BlockSpec((rows, TOP_K), lambda ib, it: (ib * nt + it, 0)),
                  pl.BlockSpec(memory_space=pl.ANY),
                  pl.BlockSpec((bb, tt, d), lambda ib, it: (ib, it, 0)),
                  pl.BlockSpec((bb, 1, d), lambda ib, it: (ib, 0, 0)),
                  pl.BlockSpec((1, 1, d), lambda ib, it: (0, 0, 0))],
        out_specs=pl.BlockSpec((bb, tt, d), lambda ib, it: (ib, it, 0)),
        scratch_shapes=[pltpu.VMEM((TOP_K, rows, d // 2), U32), pltpu.SemaphoreType.DMA(())],
        compiler_params=_cparams("arbitrary", "arbitrary"),
        name="moe_combine",
    )(dest, wt_rows, y, x1, gt, g_post.reshape(1, 1, d))


def _hier_moe(h2, eid, wt, x1, gt_f, g_post, w1, w3, w2):
    n = h2.shape[0]
    rank, cnt = _moe_rank(eid)
    counts = cnt[:, 0].astype(I32)
    padded = (counts + MOE_BLOCK - 1) // MOE_BLOCK * MOE_BLOCK
    pad_end = jnp.cumsum(padded)
    pad_start = (pad_end - padded).astype(I32)
    n_blocks = (n * TOP_K + N_EXPERTS * (MOE_BLOCK - 1) + MOE_BLOCK - 1) // MOE_BLOCK
    blk_start = jnp.arange(n_blocks, dtype=I32) * MOE_BLOCK
    blk_eid = jnp.minimum(jnp.sum((pad_end[None, :] <= blk_start[:, None]).astype(I32), axis=1), N_EXPERTS - 1)
    n_used = (pad_end[-1:] // MOE_BLOCK).astype(I32)
    experts = jnp.arange(N_EXPERTS, dtype=I32)[:, None, None]
    dest = jnp.sum(jnp.where(eid[None] == experts, pad_start[:, None, None], 0), axis=0) + rank
    xin = _moe_dispatch(dest, h2, n_blocks * MOE_BLOCK)
    y = _moe_experts(blk_eid, n_used, xin, w1, w3, w2)
    return _moe_combine(dest, jnp.swapaxes(wt, 0, 1), y, x1, gt_f, g_post)


def _layer(x, mods, shift_prev, rkv_last, s0, cache, p, w):
    b, t, d = x.shape
    ca = p["w0"].shape[-1]
    sh_a, sc_a, gt_a, sh_f, sc_f, gt_f = mods
    r, k, v, q, kb, vb = _in_proj(x, sc_a, sh_a, p["g_pre_mix"], w["w_main"], [F32, F32, F32, BF16, F32, F32],
                                  [1.0, 1.0, 1.0, Q_PRESCALE, 1.0, 1.0])
    wl, a, gate, logf, shift_new = _lora_heads(x, sc_a, sh_a, p["g_pre_mix"], shift_prev, p)
    ya, s_new = _rwkv7(r, k, v, wl, a, gate, rkv_last[..., :ca], rkv_last[..., ca:2 * ca], rkv_last[..., 2 * ca:],
                       s0, p)
    if cache is None:
        fq, _ = _forget_layouts(logf, None)
        yb = _fox_prompt(q, kb, vb, fq)
    else:
        k_past, v_past, logf_past = cache
        past = k_past.shape[1]
        fq, fk = _forget_layouts(logf, logf_past)
        yb = _fox_sample(q, kb, vb, k_past, v_past, fq, fk)
    x1, h2, eid, wt = _postmix(ya, yb, x, gt_a, sc_f, sh_f, p["g_post_mix"], p["g_pre_ffn"],
                               w["wo_a"], w["wo_b"], w["wr_hi"], w["wr_lo"], w["br"])
    out = _hier_moe(h2, eid, wt, x1, gt_f, p["g_post_ffn"], p["w1"], p["w3"], p["w2"])
    nh = logf.shape[-1]
    return (out, shift_new.reshape(b, d), s_new,
            kb.reshape(b, t, nh, HEAD_DIM), vb.reshape(b, t, nh, HEAD_DIM), logf)


def kernel(x_prompt, x_sample, state_shift, state_wkv, cache_k, cache_v, cache_logf, c_prompt, c_sample, w_ada, b_ada, g_pre_mix, g_post_mix, g_pre_ffn, g_post_ffn, w_in, b_f, mu_rkv, mu_wag, w0, w_lora_a, w_lora_b, a0, a_lora_a, a_lora_b, g_lora_a, g_lora_b, k_k, k_a, r_k, ln_x_w, ln_x_b, w_out, w_rg, b_rg, w_re, b_re, w1, w3, w2):
    stacked = dict(w_ada=w_ada, b_ada=b_ada, g_pre_mix=g_pre_mix, g_post_mix=g_post_mix, g_pre_ffn=g_pre_ffn,
                   g_post_ffn=g_post_ffn, w_in=w_in, b_f=b_f, mu_rkv=mu_rkv, mu_wag=mu_wag, w0=w0,
                   w_lora_a=w_lora_a, w_lora_b=w_lora_b, a0=a0, a_lora_a=a_lora_a, a_lora_b=a_lora_b,
                   g_lora_a=g_lora_a, g_lora_b=g_lora_b, k_k=k_k, k_a=k_a, r_k=r_k, ln_x_w=ln_x_w,
                   ln_x_b=ln_x_b, w_out=w_out, w_rg=w_rg, b_rg=b_rg, w_re=w_re, b_re=b_re, w1=w1, w3=w3, w2=w2)
    depth = w_ada.shape[0]
    assert depth == 1, "one layer per call"
    p = {name: arr[0] for name, arr in stacked.items()}
    bp, _, d = x_prompt.shape
    bs = x_sample.shape[0]
    ca = p["w0"].shape[-1]
    n_main = 3 * ca + 3 * (d - ca)
    p["w_forget"] = p["w_in"][:, n_main:]

    wr = _pad_to(jnp.concatenate([p["w_re"], p["w_rg"]], axis=1).T, 0, ROUTER_ROWS)
    wr_hi = wr.astype(BF16)
    w = dict(
        w_main=p["w_in"][:, :n_main].astype(BF16),
        wo_a=p["w_out"][:ca].astype(BF16), wo_b=p["w_out"][ca:].astype(BF16),
        wr_hi=wr_hi, wr_lo=(wr - wr_hi.astype(F32)).astype(BF16),
        br=_pad_to(jnp.concatenate([p["b_re"], p["b_rg"]]), 0, ROUTER_ROWS).reshape(ROUTER_ROWS, 1),
    )

    mod = _ada_mod(jnp.concatenate([c_prompt, c_sample], axis=0), p["w_ada"], p["b_ada"])
    mods = [mod[:, None, i * d:(i + 1) * d] for i in range(6)]
    mods_p = [m[:bp] for m in mods]
    mods_s = [m[bp:] for m in mods]

    out_p = _layer(x_prompt, mods_p, jnp.zeros((bp, 1, d), F32), jnp.zeros((bp, 1, 3 * ca), F32),
                   jnp.zeros((bp,) + state_wkv.shape[2:], F32), None, p, w)
    rkv_last = _matmul_rows(state_shift[0], w["w_main"][:, :3 * ca], tn=ca)[:, None, :]
    out_s = _layer(x_sample, mods_s, state_shift[0][:, None, :], rkv_last, state_wkv[0],
                   (cache_k[0], cache_v[0], cache_logf[0]), p, w)
    yp, shp, wkvp, kp, vp, lfp = out_p
    ys, shs, wkvs, ks_, vs_, lfs = out_s
    return (yp, ys, shp[None], wkvp[None], kp[None], vp[None], lfp[None],
            shs[None], wkvs[None], ks_[None], vs_[None], lfs[None])
```

```python
import functools
import math

import jax
import jax.numpy as jnp
from jax import lax
from jax.experimental import pallas as pl
from jax.experimental.pallas import tpu as pltpu

F32 = jnp.float32
BF16 = jnp.bfloat16
I32 = jnp.int32

HEAD_DIM = 64
LANES = 128
PAIR = 2 * HEAD_DIM
RMS_EPS = 1e-6
GN_EPS = 64e-5
N_GROUPS = 4
EXPERTS_PER_GROUP = 8
N_EXPERTS = N_GROUPS * EXPERTS_PER_GROUP
TOP_K = 2
CHUNK = 64
VMEM_LIMIT_BYTES = 56 * 1024 * 1024
ROW_TILE = 512
MOE_BLOCK = 256


def _cparams(*sem):
    return pltpu.CompilerParams(dimension_semantics=sem, vmem_limit_bytes=VMEM_LIMIT_BYTES)


def _dot(a, b):
    return jnp.dot(a, b, preferred_element_type=F32)


def _dot_nt(a, b):
    return lax.dot_general(a, b, (((1,), (1,)), ((), ())), preferred_element_type=F32)


def _dot_tn(a, b):
    return lax.dot_general(a, b, (((0,), (0,)), ((), ())), preferred_element_type=F32)


def _split3(x):
    hi = x.astype(BF16)
    r1 = x - hi.astype(F32)
    mid = r1.astype(BF16)
    lo = (r1 - mid.astype(F32)).astype(BF16)
    return hi, mid, lo


def _split2(x):
    hi = x.astype(BF16)
    return hi, (x - hi.astype(F32)).astype(BF16)


def _dot3_exact_rhs(x, m):
    hi, mid, lo = _split3(x)
    return _dot(hi, m) + _dot(mid, m) + _dot(lo, m)


def _sigmoid(x):
    return 1.0 / (1.0 + jnp.exp(-x))


def _log_sigmoid(x):
    return jnp.minimum(x, 0.0) - jnp.log(1.0 + jnp.exp(-jnp.abs(x)))


def _mod_norm(x, g, scale, shift):
    y = x * lax.rsqrt(jnp.mean(x * x, axis=-1, keepdims=True) + RMS_EPS)
    return (y * g) * (1.0 + scale) + shift


def _row_tiling(batch, seq):
    if seq >= ROW_TILE:
        assert seq % ROW_TILE == 0
        return 1, ROW_TILE
    assert ROW_TILE % seq == 0 and batch % (ROW_TILE // seq) == 0
    return ROW_TILE // seq, seq


def _ada_kernel(c_ref, w_ref, b_ref, o_ref):
    c = c_ref[...]
    s = c * _sigmoid(c)
    o_ref[...] = _dot(s.astype(BF16), w_ref[...].astype(BF16)) + b_ref[...]


def _ada_mod(c_all, w_ada, b_ada):
    m, d = c_all.shape
    n = w_ada.shape[1]
    tn = 1024
    return pl.pallas_call(
        _ada_kernel,
        out_shape=jax.ShapeDtypeStruct((m, n), F32),
        grid=(n // tn,),
        in_specs=[pl.BlockSpec((m, d), lambda j: (0, 0)),
                  pl.BlockSpec((d, tn), lambda j: (0, j)),
                  pl.BlockSpec((1, tn), lambda j: (0, j))],
        out_specs=pl.BlockSpec((m, tn), lambda j: (0, j)),
        compiler_params=_cparams("arbitrary"),
        name="ada_mod",
    )(c_all, w_ada, b_ada.reshape(1, n))


def _proj_kernel(x_ref, sc_ref, sh_ref, g_ref, w_ref, *refs, out_scales):
    outs, hs_ref = refs[:-1], refs[-1]
    j = pl.program_id(2)
    bb, tt, d = x_ref.shape

    @pl.when(j == 0)
    def _():
        h = _mod_norm(x_ref[...], g_ref[...], sc_ref[...], sh_ref[...])
        hs_ref[...] = h.reshape(bb * tt, d).astype(BF16)

    acc = _dot(hs_ref[...], w_ref[...])
    for idx, o_ref in enumerate(outs):
        @pl.when(j == idx)
        def _(o_ref=o_ref, scale=out_scales[idx]):
            scaled = acc if scale == 1.0 else acc * scale
            o_ref[...] = scaled.reshape(o_ref.shape).astype(o_ref.dtype)


def _in_proj(x, sc, sh, g, w_main, out_dtypes, out_scales):
    b, t, d = x.shape
    n_out = len(out_dtypes)
    cw = w_main.shape[1] // n_out
    bb, tt = _row_tiling(b, t)
    row_spec = pl.BlockSpec((bb, tt, d), lambda ib, it, j: (ib, it, 0))
    mod_spec = pl.BlockSpec((bb, 1, d), lambda ib, it, j: (ib, 0, 0))
    out_spec = pl.BlockSpec((bb, tt, cw), lambda ib, it, j: (ib, it, 0))
    return pl.pallas_call(
        functools.partial(_proj_kernel, out_scales=tuple(out_scales)),
        out_shape=[jax.ShapeDtypeStruct((b, t, cw), dt) for dt in out_dtypes],
        grid=(b // bb, t // tt, n_out),
        in_specs=[row_spec, mod_spec, mod_spec,
                  pl.BlockSpec((1, 1, d), lambda ib, it, j: (0, 0, 0)),
                  pl.BlockSpec((d, cw), lambda ib, it, j: (0, j))],
        out_specs=[out_spec] * n_out,
        scratch_shapes=[pltpu.VMEM((bb * tt, d), BF16)],
        compiler_params=_cparams("arbitrary", "arbitrary", "arbitrary"),
        name="in_proj",
    )(x, sc, sh, g.reshape(1, 1, d), w_main)


def _lora_kernel(x_ref, sc_ref, sh_ref, g_ref, prev_ref, muw_ref, mua_ref, mug_ref,
                 aw_ref, bw_ref, w0_ref, aa_ref, ba_ref, a0_ref, ag_ref, bg_ref, wf_ref, bf_ref,
                 wl_ref, a_ref, gate_ref, lf_ref, shn_ref, carry_ref):
    it = pl.program_id(1)
    bb, tt, d = x_ref.shape
    rows = bb * tt
    h = _mod_norm(x_ref[...], g_ref[...], sc_ref[...], sh_ref[...])

    @pl.when(it == 0)
    def _():
        carry_ref[...] = prev_ref[...]

    rolled = pltpu.roll(h.reshape(rows, d), 1, 0).reshape(bb, tt, d)
    tpos = lax.broadcasted_iota(I32, (bb, tt, d), 1)
    h_prev = jnp.where(tpos == 0, carry_ref[...], rolled)
    last = h[:, tt - 1:tt, :]
    carry_ref[...] = last
    shn_ref[...] = last

    xx = h_prev - h

    def mix(mu_ref):
        return (h + xx * mu_ref[...]).reshape(rows, d).astype(BF16)

    ca = wl_ref.shape[-1]
    lw = jnp.tanh(_dot(mix(muw_ref), aw_ref[...]))
    w = w0_ref[...] + _dot(lw.astype(BF16), bw_ref[...])
    wl_ref[...] = (-math.exp(-0.5) * _sigmoid(w)).reshape(bb, tt, ca)
    la = _dot(mix(mua_ref), aa_ref[...])
    a_ref[...] = _sigmoid(a0_ref[...] + _dot(la.astype(BF16), ba_ref[...])).reshape(bb, tt, ca)
    lg = _sigmoid(_dot(mix(mug_ref), ag_ref[...]))
    gate_ref[...] = _dot(lg.astype(BF16), bg_ref[...]).reshape(bb, tt, ca).astype(gate_ref.dtype)
    nh = lf_ref.shape[-1]
    z = _dot(h.reshape(rows, d).astype(BF16), wf_ref[...]) + bf_ref[...]
    lf_ref[...] = _log_sigmoid(z)[:, :nh].reshape(bb, tt, nh)


def _pad_to(x, axis, size):
    pad = [(0, 0)] * x.ndim
    pad[axis] = (0, size - x.shape[axis])
    return jnp.pad(x, pad)


def _round_up(n, m):
    return (n + m - 1) // m * m


def _lora_heads(x, sc, sh, g, shift_prev, p):
    b, t, d = x.shape
    ca = p["w0"].shape[-1]
    nh = p["b_f"].shape[-1]
    bb, tt = _row_tiling(b, t)

    def lora_pair(a_w, b_w):
        r = _round_up(a_w.shape[1], LANES)
        return _pad_to(a_w, 1, r).astype(BF16), _pad_to(b_w, 0, r).astype(BF16)

    aw, bw = lora_pair(p["w_lora_a"], p["w_lora_b"])
    aa, ba = lora_pair(p["a_lora_a"], p["a_lora_b"])
    ag, bg = lora_pair(p["g_lora_a"], p["g_lora_b"])
    wf = _pad_to(p["w_forget"], 1, LANES).astype(BF16)
    bf = _pad_to(p["b_f"].reshape(1, nh), 1, LANES)
    mu = p["mu_wag"].reshape(3, 1, 1, d)

    row_spec = pl.BlockSpec((bb, tt, d), lambda ib, it: (ib, it, 0))
    mod_spec = pl.BlockSpec((bb, 1, d), lambda ib, it: (ib, 0, 0))
    vec_spec = pl.BlockSpec((1, 1, d), lambda ib, it: (0, 0, 0))

    def full(a):
        return pl.BlockSpec(a.shape, lambda ib, it: (0,) * a.ndim)

    ca_spec = pl.BlockSpec((bb, tt, ca), lambda ib, it: (ib, it, 0))
    weights = [aw, bw, p["w0"].reshape(1, ca), aa, ba, p["a0"].reshape(1, ca), ag, bg, wf, bf]
    return pl.pallas_call(
        _lora_kernel,
        out_shape=[jax.ShapeDtypeStruct((b, t, ca), F32), jax.ShapeDtypeStruct((b, t, ca), F32),
                   jax.ShapeDtypeStruct((b, t, ca), BF16), jax.ShapeDtypeStruct((b, t, nh), F32),
                   jax.ShapeDtypeStruct((b, 1, d), F32)],
        grid=(b // bb, t // tt),
        in_specs=[row_spec, mod_spec, mod_spec, vec_spec, mod_spec, vec_spec, vec_spec, vec_spec]
                 + [full(a) for a in weights],
        out_specs=[ca_spec, ca_spec, ca_spec,
                   pl.BlockSpec((bb, tt, nh), lambda ib, it: (ib, it, 0)),
                   pl.BlockSpec((bb, 1, d), lambda ib, it: (ib, 0, 0))],
        scratch_shapes=[pltpu.VMEM((bb, 1, d), F32)],
        compiler_params=_cparams("arbitrary", "arbitrary"),
        name="lora_heads",
    )(x, sc, sh, g.reshape(1, 1, d), shift_prev, mu[0], mu[1], mu[2], *weights)


def _cumsum_kernel(x_ref, o_ref, carry_ref):
    it = pl.program_id(1)
    bb, nh, tl = x_ref.shape

    @pl.when(it == 0)
    def _():
        carry_ref[...] = jnp.zeros_like(carry_ref)

    r = lax.broadcasted_iota(I32, (tl, tl), 0)
    c = lax.broadcasted_iota(I32, (tl, tl), 1)
    upper = jnp.where(r <= c, 1.0, 0.0).astype(BF16)
    cs = _dot3_exact_rhs(x_ref[...].reshape(bb * nh, tl), upper) + carry_ref[...]
    o_ref[...] = cs.reshape(bb, nh, tl)
    carry_ref[...] = cs[:, tl - 1:tl]


def _cumsum_lanes(x):
    b, nh, l = x.shape
    tl = l if l <= 2304 else 1024
    assert l % tl == 0 and nh % 8 == 0
    bb = math.gcd(b, 8)
    return pl.pallas_call(
        _cumsum_kernel,
        out_shape=jax.ShapeDtypeStruct((b, nh, l), F32),
        grid=(b // bb, l // tl),
        in_specs=[pl.BlockSpec((bb, nh, tl), lambda ib, it: (ib, 0, it))],
        out_specs=pl.BlockSpec((bb, nh, tl), lambda ib, it: (ib, 0, it)),
        scratch_shapes=[pltpu.VMEM((bb * nh, 1), F32)],
        compiler_params=_cparams("arbitrary", "arbitrary"),
        name="logf_cumsum",
    )(x)


NEG_BIG = -1e30


def _head_masked(q):
    lane = lax.broadcasted_iota(I32, q.shape, 1)
    first = lane < HEAD_DIM
    zero = jnp.zeros_like(q)
    return first, (jnp.where(first, q, zero), jnp.where(first, zero, q))


FOX_TILE = 512
LOG2E = 1.0 / math.log(2.0)
Q_PRESCALE = HEAD_DIM ** -0.5 * LOG2E


def _with_bias_lanes(x, first, f_col, own_sign):
    lane = lax.broadcasted_iota(I32, x.shape, 1)
    out = []
    for e in range(2):
        keep = first if e == 0 else jnp.logical_not(first)
        base = HEAD_DIM if e == 0 else 0
        terms = [t.astype(F32) * own_sign for t in _split3(f_col[:, e:e + 1])]
        ones = jnp.ones_like(terms[0])
        cols = terms + [ones] * 3 if own_sign < 0 else [ones] * 3 + terms
        y = jnp.where(keep, x, 0.0)
        for i, col in enumerate(cols):
            y = jnp.where(lane == base + i, col, y)
        out.append(y.astype(BF16))
    return out


def _fox_prompt_kernel(q_ref, k_ref, v_ref, f_ref, o_ref, k0_ref, k1_ref, vt_ref):
    t = k_ref.shape[0]
    tq = tk = FOX_TILE
    half = HEAD_DIM
    first_t = lax.broadcasted_iota(I32, (t, PAIR), 1) < half
    kaug = _with_bias_lanes(k_ref[...], first_t, f_ref[...], -1.0)
    k0_ref[...] = kaug[0]
    k1_ref[...] = kaug[1]
    kaug_refs = (k0_ref, k1_ref)
    for i in range(t // tk):
        vt_ref[:, i * tk:(i + 1) * tk] = v_ref[i * tk:(i + 1) * tk, :].T.astype(BF16)

    first_q = lax.broadcasted_iota(I32, (tq, PAIR), 1) < half
    krow = lax.broadcasted_iota(I32, (tk, tq), 0)
    qcol = lax.broadcasted_iota(I32, (tk, tq), 1)

    def q_block(qi, carry):
        q0 = pl.multiple_of(qi * tq, tq)
        qaug = _with_bias_lanes(q_ref[pl.ds(q0, tq), :].astype(F32), first_q, f_ref[pl.ds(q0, tq), :], 1.0)

        def kv_blocks(state, blocks):
            starts = [pl.multiple_of(k0, tk) for k0, _ in blocks]
            s = [[_dot_nt(kaug_refs[e][pl.ds(k0, tk), :], qaug[e]) for e in range(2)] for k0 in starts]
            state = list(state)
            for (_, masked), k0, s_blk in zip(blocks, starts, s):
                for e in range(2):
                    m_prev, l_prev, acc = state[e]
                    s_e = jnp.where(krow <= qcol, s_blk[e], NEG_BIG) if masked else s_blk[e]
                    m_new = jnp.maximum(m_prev, jnp.max(s_e, axis=0, keepdims=True))
                    p = jnp.exp2(s_e - m_new)
                    alpha = jnp.exp2(m_prev - m_new)
                    l_new = alpha * l_prev + jnp.sum(p, axis=0, keepdims=True)
                    pv = _dot(vt_ref[e * half:(e + 1) * half, pl.ds(k0, tk)], p.astype(BF16))
                    state[e] = (m_new, l_new, acc * alpha + pv)
            return tuple(state)

        neg = jnp.full((1, tq), NEG_BIG, F32)
        zero = jnp.zeros((1, tq), F32)
        init = tuple((neg, zero, jnp.zeros((half, tq), F32)) for _ in range(2))
        state = lax.fori_loop(
            0, qi // 2, lambda i, st: kv_blocks(st, [(2 * i * tk, False), ((2 * i + 1) * tk, False)]), init)
        state = lax.cond(
            qi % 2 == 1,
            lambda st: kv_blocks(st, [((qi - 1) * tk, False), (qi * tk, True)]),
            lambda st: kv_blocks(st, [(qi * tk, True)]),
            state)
        o_t = jnp.concatenate([state[e][2] / state[e][1] for e in range(2)], axis=0)
        o_ref[pl.ds(q0, tq), :] = o_t.T.astype(o_ref.dtype)
        return carry

    lax.fori_loop(0, t // tq, q_block, 0)


def _fox_prompt(q, k, v, f_rows):
    b, t, cb = q.shape
    hp = cb // PAIR
    assert t % FOX_TILE == 0
    seq_spec = pl.BlockSpec((None, t, PAIR), lambda ib, ih: (ib, 0, ih))
    return pl.pallas_call(
        _fox_prompt_kernel,
        out_shape=jax.ShapeDtypeStruct((b, t, cb), BF16),
        grid=(b, hp),
        in_specs=[seq_spec, seq_spec, seq_spec,
                  pl.BlockSpec((None, None, t, 2), lambda ib, ih: (ib, ih, 0, 0))],
        out_specs=seq_spec,
        scratch_shapes=[pltpu.VMEM((t, PAIR), BF16), pltpu.VMEM((t, PAIR), BF16), pltpu.VMEM((PAIR, t), BF16)],
        compiler_params=_cparams("arbitrary", "arbitrary"),
        name="fox_prompt",
    )(q, k, v, f_rows)


SAMPLE_HEADS = 8


def _fox_sample_kernel(q_ref, kn_ref, vn_ref, kc_ref, vc_ref, fq_ref, fk_ref, o_ref):
    ts = q_ref.shape[0]
    nhb, n, past = kc_ref.shape
    row = lax.broadcasted_iota(I32, (ts, ts), 0)
    col = lax.broadcasted_iota(I32, (ts, ts), 1)
    for pp in range(nhb // 2):
        lanes = slice(pp * PAIR, (pp + 1) * PAIR)
        first, qs = _head_masked(q_ref[:, lanes])
        kc_t = kc_ref[2 * pp:2 * pp + 2].reshape(2 * n, past).astype(BF16)
        vc_t = vc_ref[2 * pp:2 * pp + 2].reshape(2 * n, past).astype(BF16)
        kn = kn_ref[:, lanes].astype(BF16)
        vn = vn_ref[:, lanes].astype(BF16)
        outs = []
        for e in range(2):
            fq = fq_ref[pp, :, e:e + 1]
            s_p = _dot(qs[e], kc_t) + (fq - fk_ref[pp, e:e + 1, 0:past])
            s_n = _dot_nt(qs[e], kn) + (fq - fk_ref[pp, e:e + 1, past:past + ts])
            s_n = jnp.where(col <= row, s_n, NEG_BIG)
            m = jnp.maximum(jnp.max(s_p, axis=1, keepdims=True), jnp.max(s_n, axis=1, keepdims=True))
            p_p = jnp.exp2(s_p - m)
            p_n = jnp.exp2(s_n - m)
            l = jnp.sum(p_p, axis=1, keepdims=True) + jnp.sum(p_n, axis=1, keepdims=True)
            outs.append((_dot_nt(p_p.astype(BF16), vc_t) + _dot(p_n.astype(BF16), vn)) / l)
        o_ref[:, lanes] = jnp.where(first, outs[0], outs[1]).astype(o_ref.dtype)


def _fox_sample(q, kn, vn, kc_t, vc_t, fq, fk):
    b, ts, cb = q.shape
    _, nh, n, past = kc_t.shape
    lp = fk.shape[-1]
    nhb = min(nh, SAMPLE_HEADS)
    lb = nhb * n
    new_spec = pl.BlockSpec((None, ts, lb), lambda ib, ig: (ib, 0, ig))
    past_spec = pl.BlockSpec((None, nhb, n, past), lambda ib, ig: (ib, ig, 0, 0))
    return pl.pallas_call(
        _fox_sample_kernel,
        out_shape=jax.ShapeDtypeStruct((b, ts, cb), BF16),
        grid=(b, nh // nhb),
        in_specs=[new_spec, new_spec, new_spec, past_spec, past_spec,
                  pl.BlockSpec((None, nhb // 2, ts, 2), lambda ib, ig: (ib, ig, 0, 0)),
                  pl.BlockSpec((None, nhb // 2, 2, lp), lambda ib, ig: (ib, ig, 0, 0))],
        out_specs=new_spec,
        compiler_params=_cparams("arbitrary", "arbitrary"),
        name="fox_sample",
    )(q, kn, vn, kc_t, vc_t, fq, fk)


def _forget_layouts(logf_new, logf_past):
    b, t, nh = logf_new.shape
    lt = jnp.swapaxes(logf_new, 1, 2)
    past = 0
    if logf_past is not None:
        past = logf_past.shape[1]
        lt = jnp.concatenate([jnp.swapaxes(logf_past, 1, 2), lt], axis=2)
    lp = _round_up(past + t, LANES)
    ft = _cumsum_lanes(_pad_to(lt, 2, lp)) * LOG2E
    fk = ft.reshape(b, nh // 2, 2, lp)
    fq = jnp.swapaxes(fk[..., past:past + t], 2, 3)
    return fq, fk


def _rwkv_kernel(r_ref, k_ref, v_ref, wl_ref, a_ref, g_ref, rl_ref, kl_ref, vl_ref, s0_ref,
                 mur_ref, muk_ref, muv_ref, kk_ref, ka_ref, rk_ref, lnw_ref, lnb_ref,
                 y_ref, sout_ref, st_ref, prev_ref):
    itb = pl.program_id(2)
    tb, lanes = r_ref.shape
    n_pairs = lanes // PAIR
    c = min(CHUNK, tb)

    pairs = range(n_pairs)
    heads = range(2)

    @pl.when(itb == 0)
    def _():
        zero = jnp.zeros((HEAD_DIM, HEAD_DIM), F32)
        for p in pairs:
            st_ref[p] = jnp.concatenate([jnp.concatenate([s0_ref[2 * p], zero], axis=1),
                                         jnp.concatenate([zero, s0_ref[2 * p + 1]], axis=1)], axis=0)
        prev_ref[0:1, :] = rl_ref[...]
        prev_ref[1:2, :] = kl_ref[...]
        prev_ref[2:3, :] = vl_ref[...]
    ri = lax.broadcasted_iota(I32, (c, 2 * c), 0)
    ci = lax.broadcasted_iota(I32, (c, 2 * c), 1) % c
    strict2 = ri > ci
    incl2 = ri >= ci
    tri_ones = jnp.where(incl2[:, :c], 1.0, 0.0).astype(BF16)
    li = lax.broadcasted_iota(I32, (PAIR, PAIR), 0) // HEAD_DIM
    lj = lax.broadcasted_iota(I32, (PAIR, PAIR), 1) // HEAD_DIM
    same_head = li == lj
    head_ones = jnp.where(same_head, 1.0, 0.0).astype(BF16)
    first = lax.broadcasted_iota(I32, (c, PAIR), 1) < HEAD_DIM
    row0 = lax.broadcasted_iota(I32, (c, PAIR), 0) == 0
    zeros_cb = jnp.zeros((c, PAIR), BF16)

    def head_sum2(x):
        hi, lo = _split2(x)
        s = _dot(jnp.concatenate([hi, lo], axis=0), head_ones)
        return s[:c] + s[c:]

    def shifted(x, prev_row):
        return jnp.where(row0, prev_row, pltpu.roll(x, 1, 0))

    def by_head(x):
        return [jnp.where(first, x, zeros_cb), jnp.where(first, zeros_cb, x)]

    def chunk(ic, carry):
        rows = pl.ds(pl.multiple_of(ic * c, c), c)
        lns = [slice(pp * PAIR, (pp + 1) * PAIR) for pp in pairs]
        r, k, v, wl, a, kk = [], [], [], [], [], []
        for ln in lns:
            r_raw, k_raw, v_raw = r_ref[rows, ln], k_ref[rows, ln], v_ref[rows, ln]
            r.append(r_raw + mur_ref[:, ln] * (shifted(r_raw, prev_ref[0:1, ln]) - r_raw))
            k.append(k_raw + muk_ref[:, ln] * (shifted(k_raw, prev_ref[1:2, ln]) - k_raw))
            v.append(v_raw + muv_ref[:, ln] * (shifted(v_raw, prev_ref[2:3, ln]) - v_raw))
            prev_ref[0:1, ln] = r_raw[c - 1:c, :]
            prev_ref[1:2, ln] = k_raw[c - 1:c, :]
            prev_ref[2:3, ln] = v_raw[c - 1:c, :]
            wl.append(wl_ref[rows, ln])
            a.append(a_ref[rows, ln])
            kk.append(k[-1] * kk_ref[:, ln])
        lcum = []
        for p in pairs:
            s = _dot(tri_ones, jnp.concatenate(_split3(wl[p]), axis=1))
            lcum.append(s[:, :PAIR] + s[:, PAIR:2 * PAIR] + s[:, 2 * PAIR:])
        kk_ss = [head_sum2(x * x) for x in kk]
        kf = [k[p] * (1.0 + (a[p] - 1.0) * ka_ref[:, lns[p]]) for p in pairs]
        bonus_s = [_dot((r[p] * kf[p] * rk_ref[:, lns[p]]).astype(BF16), head_ones) for p in pairs]
        kk = [kk[p] / jnp.maximum(jnp.sqrt(kk_ss[p]), 1e-12) for p in pairs]
        b = [kk[p] * a[p] for p in pairs]
        inv = [jnp.exp(-lcum[p]) for p in pairs]
        ltot = [lcum[p][c - 1:c, :] for p in pairs]
        suffix = [jnp.exp(ltot[p] - lcum[p]) for p in pairs]
        r_bar = [(r[p] * jnp.exp(lcum[p])).astype(BF16) for p in pairs]
        a_bar = [(-kk[p] * jnp.exp(lcum[p] - wl[p])).astype(BF16) for p in pairs]
        bk_bar = [jnp.concatenate([(b[p] * inv[p]).astype(BF16), (kf[p] * inv[p]).astype(BF16)], axis=0)
                  for p in pairs]
        bk_suf = [jnp.concatenate([(b[p] * suffix[p]).astype(BF16), (kf[p] * suffix[p]).astype(BF16)], axis=0)
                  for p in pairs]
        vb = [x.astype(BF16) for x in v]
        st = [st_ref[p] for p in pairs]
        stb = [x.astype(BF16) for x in st]

        prod = [_dot_nt(jnp.concatenate(by_head(a_bar[p]) + by_head(r_bar[p]), axis=0), bk_bar[p]) for p in pairs]
        a_full = [[jnp.where(strict2, prod[p][e * c:(e + 1) * c], 0.0).astype(BF16) for e in heads] for p in pairs]
        r_full = [[jnp.where(incl2, prod[p][(2 + e) * c:(3 + e) * c], 0.0).astype(BF16) for e in heads]
                  for p in pairs]
        on_s = [_dot_nt(jnp.concatenate([a_bar[p], r_bar[p]], axis=0), stb[p]) for p in pairs]
        zv = [jnp.concatenate([zeros_cb, vb[p]], axis=0) for p in pairs]
        rhs_v = [[_dot(a_full[p][e], zv[p]) for e in heads] for p in pairs]
        x = [[on_s[p][:c] + jnp.where(first, rhs_v[p][0], rhs_v[p][1])] * 2 for p in pairs]
        x = [list(xp) for xp in x]
        ap = [[a_full[p][e][:, :c] for e in heads] for p in pairs]
        span = 1
        while True:
            span *= 2
            if span >= c:
                x = [[x[p][e] + _dot(ap[p][e], x[p][e].astype(BF16)) for e in heads] for p in pairs]
                break
            res = [[_dot(ap[p][e], jnp.concatenate([x[p][e].astype(BF16), ap[p][e]], axis=1)) for e in heads]
                   for p in pairs]
            x = [[x[p][e] + res[p][e][:, :PAIR] for e in heads] for p in pairs]
            ap = [[res[p][e][:, PAIR:].astype(BF16) for e in heads] for p in pairs]
        uv = [jnp.concatenate([jnp.where(first, x[p][0], x[p][1]).astype(BF16), vb[p]], axis=0) for p in pairs]

        y_h = [[_dot(r_full[p][e], uv[p]) for e in heads] for p in pairs]
        upd = [_dot_tn(uv[p], bk_suf[p]) for p in pairs]
        for p in pairs:
            st_ref[p] = st[p] * jnp.exp(ltot[p]) + jnp.where(same_head, upd[p], 0.0)

        y = [on_s[p][c:] + jnp.where(first, y_h[p][0], y_h[p][1]) for p in pairs]
        mean = [head_sum2(y[p]) * (1.0 / HEAD_DIM) for p in pairs]
        dy = [y[p] - mean[p] for p in pairs]
        var = [head_sum2(dy[p] * dy[p]) * (1.0 / HEAD_DIM) for p in pairs]
        for p in pairs:
            ln = lns[p]
            yn = dy[p] * lax.rsqrt(var[p] + GN_EPS) * lnw_ref[:, ln] + lnb_ref[:, ln]
            y_ref[rows, ln] = ((yn + bonus_s[p] * v[p]) * g_ref[rows, ln].astype(F32)).astype(y_ref.dtype)
        return carry

    lax.fori_loop(0, tb // c, chunk, 0)

    @pl.when(itb == pl.num_programs(2) - 1)
    def _():
        for p in pairs:
            st = st_ref[p]
            sout_ref[2 * p] = st[:HEAD_DIM, :HEAD_DIM]
            sout_ref[2 * p + 1] = st[HEAD_DIM:, HEAD_DIM:]


def _rwkv7(r, k, v, wl, a, g, r_last, k_last, v_last, s0, p, pairs_per_step=8):
    b, t, ca = r.shape
    n_pairs = ca // PAIR
    pps = min(pairs_per_step, n_pairs)
    lanes = pps * PAIR
    tb = min(t, 256)
    assert t % tb == 0 and tb % min(CHUNK, tb) == 0
    seq_spec = pl.BlockSpec((None, tb, lanes), lambda ib, ig, it: (ib, it, ig))
    row_spec = pl.BlockSpec((None, 1, lanes), lambda ib, ig, it: (ib, 0, ig))
    st_spec = pl.BlockSpec((None, 2 * pps, HEAD_DIM, HEAD_DIM), lambda ib, ig, it: (ib, ig, 0, 0))
    par_spec = pl.BlockSpec((1, lanes), lambda ib, ig, it: (0, ig))
    mu = p["mu_rkv"].reshape(3, 1, ca)
    vecs = [mu[0], mu[1], mu[2], p["k_k"].reshape(1, ca), p["k_a"].reshape(1, ca), p["r_k"].reshape(1, ca),
            p["ln_x_w"].reshape(1, ca), p["ln_x_b"].reshape(1, ca)]
    return pl.pallas_call(
        _rwkv_kernel,
        out_shape=[jax.ShapeDtypeStruct((b, t, ca), BF16), jax.ShapeDtypeStruct(s0.shape, F32)],
        grid=(b, n_pairs // pps, t // tb),
        in_specs=[seq_spec] * 6 + [row_spec] * 3 + [st_spec] + [par_spec] * 8,
        out_specs=[seq_spec, st_spec],
        scratch_shapes=[pltpu.VMEM((pps, PAIR, PAIR), F32), pltpu.VMEM((8, lanes), F32)],
        compiler_params=_cparams("arbitrary", "arbitrary", "arbitrary"),
        name="rwkv7",
    )(r, k, v, wl, a, g, r_last, k_last, v_last, s0, *vecs)


def _matmul_kernel(a_ref, w_ref, o_ref):
    o_ref[...] = _dot(a_ref[...].astype(BF16), w_ref[...])


def _matmul_rows(a, w_bf16, tn=1024):
    m, kd = a.shape
    n = w_bf16.shape[1]
    assert n % tn == 0
    return pl.pallas_call(
        _matmul_kernel,
        out_shape=jax.ShapeDtypeStruct((m, n), F32),
        grid=(n // tn,),
        in_specs=[pl.BlockSpec((m, kd), lambda j: (0, 0)), pl.BlockSpec((kd, tn), lambda j: (0, j))],
        out_specs=pl.BlockSpec((m, tn), lambda j: (0, j)),
        compiler_params=_cparams("arbitrary"),
        name="matmul_rows",
    )(a, w_bf16)


ROUTER_ROWS = 40


def _rms(x, g):
    return x * lax.rsqrt(jnp.mean(x * x, axis=-1, keepdims=True) + RMS_EPS) * g


U32 = jnp.uint32
HIGH_HALF = 0xFFFF0000


def _pack_halves(x):
    n = x.shape[1] // 2
    lo = lax.bitcast_convert_type(x[:, :n].astype(BF16).astype(F32), U32)
    hi = lax.bitcast_convert_type(x[:, n:].astype(BF16).astype(F32), U32)
    return (hi & U32(HIGH_HALF)) | (lo >> U32(16))


def _unpack_halves(w):
    lo = lax.bitcast_convert_type(w << U32(16), F32)
    hi = lax.bitcast_convert_type(w & U32(HIGH_HALF), F32)
    return lo, hi


def _postmix_kernel(ya_ref, yb_ref, x_ref, gt_ref, sc_ref, sh_ref, gpost_ref, gpre_ref,
                    woa_ref, wob_ref, wrh_ref, wrl_ref, br_ref,
                    x1_ref, h2_ref, eid_ref, wt_ref):
    bb, tt, d = x_ref.shape
    rows = bb * tt
    o = (_dot(ya_ref[...].reshape(rows, ya_ref.shape[-1]), woa_ref[...])
         + _dot(yb_ref[...].reshape(rows, yb_ref.shape[-1]), wob_ref[...])).reshape(bb, tt, d)
    x1 = x_ref[...] + gt_ref[...] * _rms(o, gpost_ref[...])
    x1_ref[...] = x1
    h2 = _mod_norm(x1, gpre_ref[...], sc_ref[...], sh_ref[...]).reshape(rows, d)
    h2_ref[...] = _pack_halves(h2)

    hi = h2.astype(BF16)
    lo = (h2 - hi.astype(F32)).astype(BF16)
    logits = (_dot_nt(wrh_ref[...], hi) + _dot_nt(wrh_ref[...], lo) + _dot_nt(wrl_ref[...], hi)) + br_ref[...]
    le = logits[0:N_EXPERTS, :]
    lg = logits[N_EXPERTS:N_EXPERTS + N_GROUPS, :]
    gio = lax.broadcasted_iota(I32, lg.shape, 0)
    gmax = jnp.max(lg, axis=0, keepdims=True)
    gi = jnp.min(jnp.where(lg == gmax, gio, N_GROUPS), axis=0, keepdims=True)
    pg = 1.0 / jnp.sum(jnp.exp(lg - gmax), axis=0, keepdims=True)
    eio = lax.broadcasted_iota(I32, le.shape, 0)
    le1 = jnp.where(eio // EXPERTS_PER_GROUP == gi, le, NEG_BIG)
    m1 = jnp.max(le1, axis=0, keepdims=True)
    i1 = jnp.min(jnp.where(le1 == m1, eio, N_EXPERTS), axis=0, keepdims=True)
    le2 = jnp.where(eio == i1, NEG_BIG, le1)
    m2 = jnp.max(le2, axis=0, keepdims=True)
    i2 = jnp.min(jnp.where(le2 == m2, eio, N_EXPERTS), axis=0, keepdims=True)
    e2 = jnp.exp(m2 - m1)
    den = 1.0 + e2
    eid_ref[0:1, :] = i1
    eid_ref[1:2, :] = i2
    wt_ref[0:1, :] = pg / den
    wt_ref[1:2, :] = pg * e2 / den


def _postmix(ya, yb, x, gt, sc, sh, g_post, g_pre, wo_a, wo_b, wr_hi, wr_lo, br):
    b, t, d = x.shape
    bb, tt = _row_tiling(b, t)
    rows = bb * tt
    nt = t // tt
    n = b * t

    def row_spec(w):
        return pl.BlockSpec((bb, tt, w), lambda ib, it: (ib, it, 0))

    mod_spec = pl.BlockSpec((bb, 1, d), lambda ib, it: (ib, 0, 0))
    vec_spec = pl.BlockSpec((1, 1, d), lambda ib, it: (0, 0, 0))

    def full(a):
        return pl.BlockSpec(a.shape, lambda ib, it: (0,) * a.ndim)

    tok_spec = pl.BlockSpec((TOP_K, rows), lambda ib, it: (0, ib * nt + it))
    weights = [wo_a, wo_b, wr_hi, wr_lo, br]
    return pl.pallas_call(
        _postmix_kernel,
        out_shape=[jax.ShapeDtypeStruct((b, t, d), F32), jax.ShapeDtypeStruct((n, d // 2), U32),
                   jax.ShapeDtypeStruct((TOP_K, n), I32), jax.ShapeDtypeStruct((TOP_K, n), F32)],
        grid=(b // bb, nt),
        in_specs=[row_spec(ya.shape[-1]), row_spec(yb.shape[-1]), row_spec(d), mod_spec, mod_spec, mod_spec,
                  vec_spec, vec_spec] + [full(a) for a in weights],
        out_specs=[row_spec(d), pl.BlockSpec((rows, d // 2), lambda ib, it: (ib * nt + it, 0)), tok_spec, tok_spec],
        compiler_params=_cparams("arbitrary", "arbitrary"),
        name="postmix",
    )(ya, yb, x, gt, sc, sh, g_post.reshape(1, 1, d), g_pre.reshape(1, 1, d), *weights)


def _rank_kernel(eid_ref, rank_ref, cnt_ref, carry_ref):
    i = pl.program_id(0)
    tr = eid_ref.shape[1]

    @pl.when(i == 0)
    def _():
        carry_ref[...] = jnp.zeros_like(carry_ref)

    r = lax.broadcasted_iota(I32, (tr, tr), 0)
    c = lax.broadcasted_iota(I32, (tr, tr), 1)
    upper = jnp.where(r <= c, 1.0, 0.0).astype(BF16)
    eio = lax.broadcasted_iota(I32, (N_EXPERTS, tr), 0)
    base = carry_ref[...]
    for k in range(TOP_K):
        onehot = jnp.where(eio == eid_ref[k:k + 1, :], 1.0, 0.0)
        cum = _dot(onehot.astype(BF16), upper)
        rank = jnp.sum(onehot * (base + cum - onehot), axis=0, keepdims=True)
        rank_ref[k:k + 1, :] = rank.astype(I32)
        base = base + cum[:, tr - 1:tr]
    carry_ref[...] = base
    cnt_ref[...] = jnp.broadcast_to(base, cnt_ref.shape)


def _moe_rank(eid):
    n = eid.shape[1]
    tr = min(n, 512)
    assert n % tr == 0
    return pl.pallas_call(
        _rank_kernel,
        out_shape=[jax.ShapeDtypeStruct((TOP_K, n), I32), jax.ShapeDtypeStruct((N_EXPERTS, LANES), F32)],
        grid=(n // tr,),
        in_specs=[pl.BlockSpec((TOP_K, tr), lambda i: (0, i))],
        out_specs=[pl.BlockSpec((TOP_K, tr), lambda i: (0, i)),
                   pl.BlockSpec((N_EXPERTS, LANES), lambda i: (0, 0))],
        scratch_shapes=[pltpu.VMEM((N_EXPERTS, 1), F32)],
        compiler_params=_cparams("arbitrary"),
        name="moe_rank",
    )(eid)


def _dispatch_kernel(dest_ref, h_ref, xin0_ref, xin_ref, sem):
    del xin0_ref
    tt = h_ref.shape[0]

    def row_copy(t, dest):
        return pltpu.make_async_copy(h_ref.at[pl.ds(t, 1)], xin_ref.at[pl.ds(dest, 1)], sem)

    def issue(t, carry):
        for k in range(TOP_K):
            row_copy(t, dest_ref[k, t]).start()
        return carry

    def drain(t, carry):
        for k in range(TOP_K):
            row_copy(0, 0).wait()
        return carry

    lax.fori_loop(0, tt, issue, 0, unroll=8)
    lax.fori_loop(0, tt, drain, 0, unroll=8)


def _moe_dispatch(dest, h2, n_rows):
    n, d = h2.shape
    tt = min(n, MOE_BLOCK)
    return pl.pallas_call(
        _dispatch_kernel,
        out_shape=jax.ShapeDtypeStruct((n_rows, d), h2.dtype),
        grid=(n // tt,),
        in_specs=[pl.BlockSpec((TOP_K, tt), lambda i: (0, i), memory_space=pltpu.SMEM),
                  pl.BlockSpec((tt, d), lambda i: (i, 0)),
                  pl.BlockSpec(memory_space=pl.ANY)],
        out_specs=pl.BlockSpec(memory_space=pl.ANY),
        scratch_shapes=[pltpu.SemaphoreType.DMA(())],
        input_output_aliases={2: 0},
        compiler_params=_cparams("arbitrary"),
        name="moe_dispatch",
    )(dest, h2, jnp.zeros((n_rows, d), h2.dtype))


def _expert_kernel(be_ref, nu_ref, x_ref, w1_ref, w3_ref, w2_ref, y_ref, w1b_ref, w3b_ref, w2b_ref):
    i = pl.program_id(0)

    @pl.when(jnp.logical_or(i == 0, be_ref[i] != be_ref[jnp.maximum(i - 1, 0)]))
    def _():
        w1b_ref[...] = w1_ref[...].astype(BF16)
        w3b_ref[...] = w3_ref[...].astype(BF16)
        w2b_ref[...] = w2_ref[...].astype(BF16)

    @pl.when(i < nu_ref[0])
    def _():
        x_lo, x_hi = _unpack_halves(x_ref[...])
        n = x_lo.shape[1]
        x_lo, x_hi = x_lo.astype(BF16), x_hi.astype(BF16)
        h1 = _dot(x_lo, w1b_ref[0:n, :]) + _dot(x_hi, w1b_ref[n:2 * n, :])
        h3 = _dot(x_lo, w3b_ref[0:n, :]) + _dot(x_hi, w3b_ref[n:2 * n, :])
        hb = (h1 * _sigmoid(h1)) * h3
        y_ref[...] = _pack_halves(_dot(hb.astype(BF16), w2b_ref[...]))

    @pl.when(i >= nu_ref[0])
    def _():
        y_ref[...] = jnp.zeros_like(y_ref)


def _moe_experts(blk_eid, n_used, xin, w1, w3, w2):
    n_rows, dh = xin.shape
    e, f, d = w2.shape
    up_spec = pl.BlockSpec((None, d, f), lambda i, be, nu: (be[i], 0, 0))
    grid_spec = pltpu.PrefetchScalarGridSpec(
        num_scalar_prefetch=2,
        grid=(n_rows // MOE_BLOCK,),
        in_specs=[pl.BlockSpec((MOE_BLOCK, dh), lambda i, be, nu: (i, 0)), up_spec, up_spec,
                  pl.BlockSpec((None, f, d), lambda i, be, nu: (be[i], 0, 0))],
        out_specs=pl.BlockSpec((MOE_BLOCK, dh), lambda i, be, nu: (i, 0)),
        scratch_shapes=[pltpu.VMEM((d, f), BF16), pltpu.VMEM((d, f), BF16), pltpu.VMEM((f, d), BF16)],
    )
    return pl.pallas_call(
        _expert_kernel,
        out_shape=jax.ShapeDtypeStruct((n_rows, dh), U32),
        grid_spec=grid_spec,
        compiler_params=_cparams("arbitrary"),
        name="moe_experts",
    )(blk_eid, n_used, xin, w1, w3, w2)


def _combine_kernel(dest_ref, wt_ref, y_ref, x1_ref, gt_ref, g_ref, o_ref, buf_ref, sem):
    bb, tt, d = x1_ref.shape
    rows = bb * tt

    def row_copy(k, t, src):
        return pltpu.make_async_copy(y_ref.at[pl.ds(src, 1)], buf_ref.at[k, pl.ds(t, 1)], sem)

    def issue(t, carry):
        for k in range(TOP_K):
            row_copy(k, t, dest_ref[k, t]).start()
        return carry

    def drain(t, carry):
        for k in range(TOP_K):
            row_copy(k, 0, 0).wait()
        return carry

    lax.fori_loop(0, rows, issue, 0, unroll=8)
    lax.fori_loop(0, rows, drain, 0, unroll=8)
    wt = wt_ref[...]
    lo0, hi0 = _unpack_halves(buf_ref[0])
    lo1, hi1 = _unpack_halves(buf_ref[1])
    w0, w1 = wt[:, 0:1], wt[:, 1:2]
    f = jnp.concatenate([lo0 * w0 + lo1 * w1, hi0 * w0 + hi1 * w1], axis=1).reshape(bb, tt, d)
    o_ref[...] = x1_ref[...] + gt_ref[...] * _rms(f, g_ref[...])


def _moe_combine(dest, wt_rows, y, x1, gt, g_post):
    b, t, d = x1.shape
    if t >= MOE_BLOCK:
        bb, tt = 1, MOE_BLOCK
    else:
        bb, tt = MOE_BLOCK // t, t
    assert t % tt == 0 and b % bb == 0
    rows = bb * tt
    nt = t // tt
    return pl.pallas_call(
        _combine_kernel,
        out_shape=jax.ShapeDtypeStruct((b, t, d), F32),
        grid=(b // bb, nt),
        in_specs=[pl.BlockSpec((TOP_K, rows), lambda ib, it: (0, ib * nt + it), memory_space=pltpu.SMEM),
                  pl.BlockSpec((rows, TOP_K), lambda ib, it: (ib * nt + it, 0)),
                  pl.BlockSpec(memory_space=pl.ANY),
                  pl.BlockSpec((bb, tt, d), lambda ib, it: (ib, it, 0)),
                  pl.BlockSpec((bb, 1, d), lambda ib, it: (ib, 0, 0)),
                  pl.BlockSpec((1, 1, d), lambda ib, it: (0, 0, 0))],
        out_specs=pl.BlockSpec((bb, tt, d), lambda ib, it: (ib, it, 0)),
        scratch_shapes=[pltpu.VMEM((TOP_K, rows, d // 2), U32), pltpu.SemaphoreType.DMA(())],
        compiler_params=_cparams("arbitrary", "arbitrary"),
        name="moe_combine",
    )(dest, wt_rows, y, x1, gt, g_post.reshape(1, 1, d))


def _hier_moe(h2, eid, wt, x1, gt_f, g_post, w1, w3, w2):
    n = h2.shape[0]
    rank, cnt = _moe_rank(eid)
    counts = cnt[:, 0].astype(I32)
    padded = (counts + MOE_BLOCK - 1) // MOE_BLOCK * MOE_BLOCK
    pad_end = jnp.cumsum(padded)
    pad_start = (pad_end - padded).astype(I32)
    n_blocks = (n * TOP_K + N_EXPERTS * (MOE_BLOCK - 1) + MOE_BLOCK - 1) // MOE_BLOCK
    blk_start = jnp.arange(n_blocks, dtype=I32) * MOE_BLOCK
    blk_eid = jnp.minimum(jnp.sum((pad_end[None, :] <= blk_start[:, None]).astype(I32), axis=1), N_EXPERTS - 1)
    n_used = (pad_end[-1:] // MOE_BLOCK).astype(I32)
    experts = jnp.arange(N_EXPERTS, dtype=I32)[:, None, None]
    dest = jnp.sum(jnp.where(eid[None] == experts, pad_start[:, None, None], 0), axis=0) + rank
    xin = _moe_dispatch(dest, h2, n_blocks * MOE_BLOCK)
    y = _moe_experts(blk_eid, n_used, xin, w1, w3, w2)
    return _moe_combine(dest, jnp.swapaxes(wt, 0, 1), y, x1, gt_f, g_post)


def _layer(x, mods, shift_prev, rkv_last, s0, cache, p, w):
    b, t, d = x.shape
    ca = p["w0"].shape[-1]
    sh_a, sc_a, gt_a, sh_f, sc_f, gt_f = mods
    r, k, v, q, kb, vb = _in_proj(x, sc_a, sh_a, p["g_pre_mix"], w["w_main"], [F32, F32, F32, BF16, F32, F32],
                                  [1.0, 1.0, 1.0, Q_PRESCALE, 1.0, 1.0])
    wl, a, gate, logf, shift_new = _lora_heads(x, sc_a, sh_a, p["g_pre_mix"], shift_prev, p)
    ya, s_new = _rwkv7(r, k, v, wl, a, gate, rkv_last[..., :ca], rkv_last[..., ca:2 * ca], rkv_last[..., 2 * ca:],
                       s0, p)
    if cache is None:
        fq, _ = _forget_layouts(logf, None)
        yb = _fox_prompt(q, kb, vb, fq)
    else:
        k_past, v_past, logf_past = cache
        past = k_past.shape[1]
        fq, fk = _forget_layouts(logf, logf_past)
        yb = _fox_sample(q, kb, vb, jnp.transpose(k_past, (0, 2, 3, 1)), jnp.transpose(v_past, (0, 2, 3, 1)), fq, fk)
    x1, h2, eid, wt = _postmix(ya, yb, x, gt_a, sc_f, sh_f, p["g_post_mix"], p["g_pre_ffn"],
                               w["wo_a"], w["wo_b"], w["wr_hi"], w["wr_lo"], w["br"])
    out = _hier_moe(h2, eid, wt, x1, gt_f, p["g_post_ffn"], p["w1"], p["w3"], p["w2"])
    nh = logf.shape[-1]
    return (out, shift_new.reshape(b, d), s_new,
            kb.reshape(b, t, nh, HEAD_DIM), vb.reshape(b, t, nh, HEAD_DIM), logf)


def kernel(x_prompt, x_sample, state_shift, state_wkv, cache_k, cache_v, cache_logf, c_prompt, c_sample, w_ada, b_ada, g_pre_mix, g_post_mix, g_pre_ffn, g_post_ffn, w_in, b_f, mu_rkv, mu_wag, w0, w_lora_a, w_lora_b, a0, a_lora_a, a_lora_b, g_lora_a, g_lora_b, k_k, k_a, r_k, ln_x_w, ln_x_b, w_out, w_rg, b_rg, w_re, b_re, w1, w3, w2):
    stacked = dict(w_ada=w_ada, b_ada=b_ada, g_pre_mix=g_pre_mix, g_post_mix=g_post_mix, g_pre_ffn=g_pre_ffn,
                   g_post_ffn=g_post_ffn, w_in=w_in, b_f=b_f, mu_rkv=mu_rkv, mu_wag=mu_wag, w0=w0,
                   w_lora_a=w_lora_a, w_lora_b=w_lora_b, a0=a0, a_lora_a=a_lora_a, a_lora_b=a_lora_b,
                   g_lora_a=g_lora_a, g_lora_b=g_lora_b, k_k=k_k, k_a=k_a, r_k=r_k, ln_x_w=ln_x_w,
                   ln_x_b=ln_x_b, w_out=w_out, w_rg=w_rg, b_rg=b_rg, w_re=w_re, b_re=b_re, w1=w1, w3=w3, w2=w2)
    depth = w_ada.shape[0]
    assert depth == 1, "one layer per call"
    p = {name: arr[0] for name, arr in stacked.items()}
    bp, _, d = x_prompt.shape
    bs = x_sample.shape[0]
    ca = p["w0"].shape[-1]
    n_main = 3 * ca + 3 * (d - ca)
    p["w_forget"] = p["w_in"][:, n_main:]

    wr = _pad_to(jnp.concatenate([p["w_re"], p["w_rg"]], axis=1).T, 0, ROUTER_ROWS)
    wr_hi = wr.astype(BF16)
    w = dict(
        w_main=p["w_in"][:, :n_main].astype(BF16),
        wo_a=p["w_out"][:ca].astype(BF16), wo_b=p["w_out"][ca:].astype(BF16),
        wr_hi=wr_hi, wr_lo=(wr - wr_hi.astype(F32)).astype(BF16),
        br=_pad_to(jnp.concatenate([p["b_re"], p["b_rg"]]), 0, ROUTER_ROWS).reshape(ROUTER_ROWS, 1),
    )

    mod = _ada_mod(jnp.concatenate([c_prompt, c_sample], axis=0), p["w_ada"], p["b_ada"])
    mods = [mod[:, None, i * d:(i + 1) * d] for i in range(6)]
    mods_p = [m[:bp] for m in mods]
    mods_s = [m[bp:] for m in mods]

    out_p = _layer(x_prompt, mods_p, jnp.zeros((bp, 1, d), F32), jnp.zeros((bp, 1, 3 * ca), F32),
                   jnp.zeros((bp,) + state_wkv.shape[2:], F32), None, p, w)
    rkv_last = _matmul_rows(state_shift[0], w["w_main"][:, :3 * ca], tn=ca)[:, None, :]
    out_s = _layer(x_sample, mods_s, state_shift[0][:, None, :], rkv_last, state_wkv[0],
                   (cache_k[0], cache_v[0], cache_logf[0]), p, w)
    yp, shp, wkvp, kp, vp, lfp = out_p
    ys, shs, wkvs, ks_, vs_, lfs = out_s
    return (yp, ys, shp[None], wkvp[None], kp[None], vp[None], lfp[None],
            shs[None], wkvs[None], ks_[None], vs_[None], lfs[None])
```

```python
import functools
import math

import jax
import jax.numpy as jnp
from jax import lax
from jax.experimental import pallas as pl
from jax.experimental.pallas import tpu as pltpu

F32 = jnp.float32
BF16 = jnp.bfloat16
I32 = jnp.int32

HEAD_DIM = 64
LANES = 128
PAIR = 2 * HEAD_DIM
RMS_EPS = 1e-6
GN_EPS = 64e-5
N_GROUPS = 4
EXPERTS_PER_GROUP = 8
N_EXPERTS = N_GROUPS * EXPERTS_PER_GROUP
TOP_K = 2
CHUNK = 64
VMEM_LIMIT_BYTES = 56 * 1024 * 1024
ROW_TILE = 512
MOE_BLOCK = 256


def _cparams(*sem):
    return pltpu.CompilerParams(dimension_semantics=sem, vmem_limit_bytes=VMEM_LIMIT_BYTES)


def _dot(a, b):
    return jnp.dot(a, b, preferred_element_type=F32)


def _dot_nt(a, b):
    return lax.dot_general(a, b, (((1,), (1,)), ((), ())), preferred_element_type=F32)


def _dot_tn(a, b):
    return lax.dot_general(a, b, (((0,), (0,)), ((), ())), preferred_element_type=F32)


def _split3(x):
    hi = x.astype(BF16)
    r1 = x - hi.astype(F32)
    mid = r1.astype(BF16)
    lo = (r1 - mid.astype(F32)).astype(BF16)
    return hi, mid, lo


def _split2(x):
    hi = x.astype(BF16)
    return hi, (x - hi.astype(F32)).astype(BF16)


def _dot3_exact_rhs(x, m):
    hi, mid, lo = _split3(x)
    return _dot(hi, m) + _dot(mid, m) + _dot(lo, m)


def _sigmoid(x):
    return 1.0 / (1.0 + jnp.exp(-x))


def _log_sigmoid(x):
    return jnp.minimum(x, 0.0) - jnp.log(1.0 + jnp.exp(-jnp.abs(x)))


def _mod_norm(x, g, scale, shift):
    y = x * lax.rsqrt(jnp.mean(x * x, axis=-1, keepdims=True) + RMS_EPS)
    return (y * g) * (1.0 + scale) + shift


def _row_tiling(batch, seq):
    if seq >= ROW_TILE:
        assert seq % ROW_TILE == 0
        return 1, ROW_TILE
    assert ROW_TILE % seq == 0 and batch % (ROW_TILE // seq) == 0
    return ROW_TILE // seq, seq


def _ada_kernel(c_ref, w_ref, b_ref, o_ref):
    c = c_ref[...]
    s = c * _sigmoid(c)
    o_ref[...] = _dot(s.astype(BF16), w_ref[...].astype(BF16)) + b_ref[...]


def _ada_mod(c_all, w_ada, b_ada):
    m, d = c_all.shape
    n = w_ada.shape[1]
    tn = 1024
    return pl.pallas_call(
        _ada_kernel,
        out_shape=jax.ShapeDtypeStruct((m, n), F32),
        grid=(n // tn,),
        in_specs=[pl.BlockSpec((m, d), lambda j: (0, 0)),
                  pl.BlockSpec((d, tn), lambda j: (0, j)),
                  pl.BlockSpec((1, tn), lambda j: (0, j))],
        out_specs=pl.BlockSpec((m, tn), lambda j: (0, j)),
        compiler_params=_cparams("arbitrary"),
        name="ada_mod",
    )(c_all, w_ada, b_ada.reshape(1, n))


def _proj_kernel(x_ref, sc_ref, sh_ref, g_ref, w_ref, *refs, out_scales):
    outs, hs_ref = refs[:-1], refs[-1]
    j = pl.program_id(2)
    bb, tt, d = x_ref.shape

    @pl.when(j == 0)
    def _():
        h = _mod_norm(x_ref[...], g_ref[...], sc_ref[...], sh_ref[...])
        hs_ref[...] = h.reshape(bb * tt, d).astype(BF16)

    acc = _dot(hs_ref[...], w_ref[...])
    for idx, o_ref in enumerate(outs):
        @pl.when(j == idx)
        def _(o_ref=o_ref, scale=out_scales[idx]):
            scaled = acc if scale == 1.0 else acc * scale
            o_ref[...] = scaled.reshape(o_ref.shape).astype(o_ref.dtype)


def _in_proj(x, sc, sh, g, w_main, out_dtypes, out_scales):
    b, t, d = x.shape
    n_out = len(out_dtypes)
    cw = w_main.shape[1] // n_out
    bb, tt = _row_tiling(b, t)
    row_spec = pl.BlockSpec((bb, tt, d), lambda ib, it, j: (ib, it, 0))
    mod_spec = pl.BlockSpec((bb, 1, d), lambda ib, it, j: (ib, 0, 0))
    out_spec = pl.BlockSpec((bb, tt, cw), lambda ib, it, j: (ib, it, 0))
    return pl.pallas_call(
        functools.partial(_proj_kernel, out_scales=tuple(out_scales)),
        out_shape=[jax.ShapeDtypeStruct((b, t, cw), dt) for dt in out_dtypes],
        grid=(b // bb, t // tt, n_out),
        in_specs=[row_spec, mod_spec, mod_spec,
                  pl.BlockSpec((1, 1, d), lambda ib, it, j: (0, 0, 0)),
                  pl.BlockSpec((d, cw), lambda ib, it, j: (0, j))],
        out_specs=[out_spec] * n_out,
        scratch_shapes=[pltpu.VMEM((bb * tt, d), BF16)],
        compiler_params=_cparams("arbitrary", "arbitrary", "arbitrary"),
        name="in_proj",
    )(x, sc, sh, g.reshape(1, 1, d), w_main)


def _lora_kernel(x_ref, sc_ref, sh_ref, g_ref, prev_ref, muw_ref, mua_ref, mug_ref,
                 aw_ref, bw_ref, w0_ref, aa_ref, ba_ref, a0_ref, ag_ref, bg_ref, wf_ref, bf_ref,
                 wl_ref, a_ref, gate_ref, lf_ref, shn_ref, carry_ref):
    it = pl.program_id(1)
    bb, tt, d = x_ref.shape
    rows = bb * tt
    h = _mod_norm(x_ref[...], g_ref[...], sc_ref[...], sh_ref[...])

    @pl.when(it == 0)
    def _():
        carry_ref[...] = prev_ref[...]

    rolled = pltpu.roll(h.reshape(rows, d), 1, 0).reshape(bb, tt, d)
    tpos = lax.broadcasted_iota(I32, (bb, tt, d), 1)
    h_prev = jnp.where(tpos == 0, carry_ref[...], rolled)
    last = h[:, tt - 1:tt, :]
    carry_ref[...] = last
    shn_ref[...] = last

    xx = h_prev - h

    def mix(mu_ref):
        return (h + xx * mu_ref[...]).reshape(rows, d).astype(BF16)

    ca = wl_ref.shape[-1]
    lw = jnp.tanh(_dot(mix(muw_ref), aw_ref[...]))
    w = w0_ref[...] + _dot(lw.astype(BF16), bw_ref[...])
    wl_ref[...] = (-math.exp(-0.5) * _sigmoid(w)).reshape(bb, tt, ca)
    la = _dot(mix(mua_ref), aa_ref[...])
    a_ref[...] = _sigmoid(a0_ref[...] + _dot(la.astype(BF16), ba_ref[...])).reshape(bb, tt, ca)
    lg = _sigmoid(_dot(mix(mug_ref), ag_ref[...]))
    gate_ref[...] = _dot(lg.astype(BF16), bg_ref[...]).reshape(bb, tt, ca).astype(gate_ref.dtype)
    nh = lf_ref.shape[-1]
    z = _dot(h.reshape(rows, d).astype(BF16), wf_ref[...]) + bf_ref[...]
    lf_ref[...] = _log_sigmoid(z)[:, :nh].reshape(bb, tt, nh)


def _pad_to(x, axis, size):
    pad = [(0, 0)] * x.ndim
    pad[axis] = (0, size - x.shape[axis])
    return jnp.pad(x, pad)


def _round_up(n, m):
    return (n + m - 1) // m * m


def _lora_heads(x, sc, sh, g, shift_prev, p):
    b, t, d = x.shape
    ca = p["w0"].shape[-1]
    nh = p["b_f"].shape[-1]
    bb, tt = _row_tiling(b, t)

    def lora_pair(a_w, b_w):
        r = _round_up(a_w.shape[1], LANES)
        return _pad_to(a_w, 1, r).astype(BF16), _pad_to(b_w, 0, r).astype(BF16)

    aw, bw = lora_pair(p["w_lora_a"], p["w_lora_b"])
    aa, ba = lora_pair(p["a_lora_a"], p["a_lora_b"])
    ag, bg = lora_pair(p["g_lora_a"], p["g_lora_b"])
    wf = _pad_to(p["w_forget"], 1, LANES).astype(BF16)
    bf = _pad_to(p["b_f"].reshape(1, nh), 1, LANES)
    mu = p["mu_wag"].reshape(3, 1, 1, d)

    row_spec = pl.BlockSpec((bb, tt, d), lambda ib, it: (ib, it, 0))
    mod_spec = pl.BlockSpec((bb, 1, d), lambda ib, it: (ib, 0, 0))
    vec_spec = pl.BlockSpec((1, 1, d), lambda ib, it: (0, 0, 0))

    def full(a):
        return pl.BlockSpec(a.shape, lambda ib, it: (0,) * a.ndim)

    ca_spec = pl.BlockSpec((bb, tt, ca), lambda ib, it: (ib, it, 0))
    weights = [aw, bw, p["w0"].reshape(1, ca), aa, ba, p["a0"].reshape(1, ca), ag, bg, wf, bf]
    return pl.pallas_call(
        _lora_kernel,
        out_shape=[jax.ShapeDtypeStruct((b, t, ca), F32), jax.ShapeDtypeStruct((b, t, ca), F32),
                   jax.ShapeDtypeStruct((b, t, ca), BF16), jax.ShapeDtypeStruct((b, t, nh), F32),
                   jax.ShapeDtypeStruct((b, 1, d), F32)],
        grid=(b // bb, t // tt),
        in_specs=[row_spec, mod_spec, mod_spec, vec_spec, mod_spec, vec_spec, vec_spec, vec_spec]
                 + [full(a) for a in weights],
        out_specs=[ca_spec, ca_spec, ca_spec,
                   pl.BlockSpec((bb, tt, nh), lambda ib, it: (ib, it, 0)),
                   pl.BlockSpec((bb, 1, d), lambda ib, it: (ib, 0, 0))],
        scratch_shapes=[pltpu.VMEM((bb, 1, d), F32)],
        compiler_params=_cparams("arbitrary", "arbitrary"),
        name="lora_heads",
    )(x, sc, sh, g.reshape(1, 1, d), shift_prev, mu[0], mu[1], mu[2], *weights)


def _cumsum_kernel(x_ref, o_ref, carry_ref):
    it = pl.program_id(1)
    bb, nh, tl = x_ref.shape

    @pl.when(it == 0)
    def _():
        carry_ref[...] = jnp.zeros_like(carry_ref)

    r = lax.broadcasted_iota(I32, (tl, tl), 0)
    c = lax.broadcasted_iota(I32, (tl, tl), 1)
    upper = jnp.where(r <= c, 1.0, 0.0).astype(BF16)
    cs = _dot3_exact_rhs(x_ref[...].reshape(bb * nh, tl), upper) + carry_ref[...]
    o_ref[...] = cs.reshape(bb, nh, tl)
    carry_ref[...] = cs[:, tl - 1:tl]


def _cumsum_lanes(x):
    b, nh, l = x.shape
    tl = l if l <= 2304 else 1024
    assert l % tl == 0 and nh % 8 == 0
    bb = math.gcd(b, 8)
    return pl.pallas_call(
        _cumsum_kernel,
        out_shape=jax.ShapeDtypeStruct((b, nh, l), F32),
        grid=(b // bb, l // tl),
        in_specs=[pl.BlockSpec((bb, nh, tl), lambda ib, it: (ib, 0, it))],
        out_specs=pl.BlockSpec((bb, nh, tl), lambda ib, it: (ib, 0, it)),
        scratch_shapes=[pltpu.VMEM((bb * nh, 1), F32)],
        compiler_params=_cparams("arbitrary", "arbitrary"),
        name="logf_cumsum",
    )(x)


NEG_BIG = -1e30


def _head_masked(q):
    lane = lax.broadcasted_iota(I32, q.shape, 1)
    first = lane < HEAD_DIM
    zero = jnp.zeros_like(q)
    return first, (jnp.where(first, q, zero), jnp.where(first, zero, q))


FOX_TILE = 512
LOG2E = 1.0 / math.log(2.0)
Q_PRESCALE = HEAD_DIM ** -0.5 * LOG2E


def _with_bias_lanes(x, first, f_col, own_sign):
    lane = lax.broadcasted_iota(I32, x.shape, 1)
    out = []
    for e in range(2):
        keep = first if e == 0 else jnp.logical_not(first)
        base = HEAD_DIM if e == 0 else 0
        terms = [t.astype(F32) * own_sign for t in _split3(f_col[:, e:e + 1])]
        ones = jnp.ones_like(terms[0])
        cols = terms + [ones] * 3 if own_sign < 0 else [ones] * 3 + terms
        y = jnp.where(keep, x, 0.0)
        for i, col in enumerate(cols):
            y = jnp.where(lane == base + i, col, y)
        out.append(y.astype(BF16))
    return out


def _fox_prompt_kernel(q_ref, k_ref, v_ref, f_ref, o_ref, k0_ref, k1_ref, vt_ref):
    t = k_ref.shape[0]
    tq = tk = FOX_TILE
    half = HEAD_DIM
    first_t = lax.broadcasted_iota(I32, (t, PAIR), 1) < half
    kaug = _with_bias_lanes(k_ref[...], first_t, f_ref[...], -1.0)
    k0_ref[...] = kaug[0]
    k1_ref[...] = kaug[1]
    kaug_refs = (k0_ref, k1_ref)
    for i in range(t // tk):
        vt_ref[:, i * tk:(i + 1) * tk] = v_ref[i * tk:(i + 1) * tk, :].T.astype(BF16)

    first_q = lax.broadcasted_iota(I32, (tq, PAIR), 1) < half
    krow = lax.broadcasted_iota(I32, (tk, tq), 0)
    qcol = lax.broadcasted_iota(I32, (tk, tq), 1)

    def q_block(qi, carry):
        q0 = pl.multiple_of(qi * tq, tq)
        qaug = _with_bias_lanes(q_ref[pl.ds(q0, tq), :].astype(F32), first_q, f_ref[pl.ds(q0, tq), :], 1.0)

        def kv_blocks(state, blocks):
            starts = [pl.multiple_of(k0, tk) for k0, _ in blocks]
            s = [[_dot_nt(kaug_refs[e][pl.ds(k0, tk), :], qaug[e]) for e in range(2)] for k0 in starts]
            state = list(state)
            for (_, masked), k0, s_blk in zip(blocks, starts, s):
                for e in range(2):
                    m_prev, l_prev, acc = state[e]
                    s_e = jnp.where(krow <= qcol, s_blk[e], NEG_BIG) if masked else s_blk[e]
                    m_new = jnp.maximum(m_prev, jnp.max(s_e, axis=0, keepdims=True))
                    p = jnp.exp2(s_e - m_new)
                    alpha = jnp.exp2(m_prev - m_new)
                    l_new = alpha * l_prev + jnp.sum(p, axis=0, keepdims=True)
                    pv = _dot(vt_ref[e * half:(e + 1) * half, pl.ds(k0, tk)], p.astype(BF16))
                    state[e] = (m_new, l_new, acc * alpha + pv)
            return tuple(state)

        neg = jnp.full((1, tq), NEG_BIG, F32)
        zero = jnp.zeros((1, tq), F32)
        init = tuple((neg, zero, jnp.zeros((half, tq), F32)) for _ in range(2))
        state = lax.fori_loop(
            0, qi // 2, lambda i, st: kv_blocks(st, [(2 * i * tk, False), ((2 * i + 1) * tk, False)]), init)
        state = lax.cond(
            qi % 2 == 1,
            lambda st: kv_blocks(st, [((qi - 1) * tk, False), (qi * tk, True)]),
            lambda st: kv_blocks(st, [(qi * tk, True)]),
            state)
        o_t = jnp.concatenate([state[e][2] / state[e][1] for e in range(2)], axis=0)
        o_ref[pl.ds(q0, tq), :] = o_t.T.astype(o_ref.dtype)
        return carry

    lax.fori_loop(0, t // tq, q_block, 0)


def _fox_prompt(q, k, v, f_rows):
    b, t, cb = q.shape
    hp = cb // PAIR
    assert t % FOX_TILE == 0
    seq_spec = pl.BlockSpec((None, t, PAIR), lambda ib, ih: (ib, 0, ih))
    return pl.pallas_call(
        _fox_prompt_kernel,
        out_shape=jax.ShapeDtypeStruct((b, t, cb), BF16),
        grid=(b, hp),
        in_specs=[seq_spec, seq_spec, seq_spec,
                  pl.BlockSpec((None, None, t, 2), lambda ib, ih: (ib, ih, 0, 0))],
        out_specs=seq_spec,
        scratch_shapes=[pltpu.VMEM((t, PAIR), BF16), pltpu.VMEM((t, PAIR), BF16), pltpu.VMEM((PAIR, t), BF16)],
        compiler_params=_cparams("arbitrary", "arbitrary"),
        name="fox_prompt",
    )(q, k, v, f_rows)


SAMPLE_HEADS = 8


def _fox_sample_kernel(q_ref, kn_ref, vn_ref, kc_ref, vc_ref, fq_ref, fk_ref, o_ref):
    ts = q_ref.shape[0]
    nhb, n, past = kc_ref.shape
    row = lax.broadcasted_iota(I32, (ts, ts), 0)
    col = lax.broadcasted_iota(I32, (ts, ts), 1)
    for pp in range(nhb // 2):
        lanes = slice(pp * PAIR, (pp + 1) * PAIR)
        first, qs = _head_masked(q_ref[:, lanes])
        kc_t = kc_ref[2 * pp:2 * pp + 2].reshape(2 * n, past).astype(BF16)
        vc_t = vc_ref[2 * pp:2 * pp + 2].reshape(2 * n, past).astype(BF16)
        kn = kn_ref[:, lanes].astype(BF16)
        vn = vn_ref[:, lanes].astype(BF16)
        outs = []
        for e in range(2):
            fq = fq_ref[pp, :, e:e + 1]
            s_p = _dot(qs[e], kc_t) + (fq - fk_ref[pp, e:e + 1, 0:past])
            s_n = _dot_nt(qs[e], kn) + (fq - fk_ref[pp, e:e + 1, past:past + ts])
            s_n = jnp.where(col <= row, s_n, NEG_BIG)
            m = jnp.maximum(jnp.max(s_p, axis=1, keepdims=True), jnp.max(s_n, axis=1, keepdims=True))
            p_p = jnp.exp2(s_p - m)
            p_n = jnp.exp2(s_n - m)
            l = jnp.sum(p_p, axis=1, keepdims=True) + jnp.sum(p_n, axis=1, keepdims=True)
            outs.append((_dot_nt(p_p.astype(BF16), vc_t) + _dot(p_n.astype(BF16), vn)) / l)
        o_ref[:, lanes] = jnp.where(first, outs[0], outs[1]).astype(o_ref.dtype)


def _fox_sample(q, kn, vn, kc_t, vc_t, fq, fk):
    b, ts, cb = q.shape
    _, nh, n, past = kc_t.shape
    lp = fk.shape[-1]
    nhb = min(nh, SAMPLE_HEADS)
    lb = nhb * n
    new_spec = pl.BlockSpec((None, ts, lb), lambda ib, ig: (ib, 0, ig))
    past_spec = pl.BlockSpec((None, nhb, n, past), lambda ib, ig: (ib, ig, 0, 0))
    return pl.pallas_call(
        _fox_sample_kernel,
        out_shape=jax.ShapeDtypeStruct((b, ts, cb), BF16),
        grid=(b, nh // nhb),
        in_specs=[new_spec, new_spec, new_spec, past_spec, past_spec,
                  pl.BlockSpec((None, nhb // 2, ts, 2), lambda ib, ig: (ib, ig, 0, 0)),
                  pl.BlockSpec((None, nhb // 2, 2, lp), lambda ib, ig: (ib, ig, 0, 0))],
        out_specs=new_spec,
        compiler_params=_cparams("arbitrary", "arbitrary"),
        name="fox_sample",
    )(q, kn, vn, kc_t, vc_t, fq, fk)


def _forget_layouts(logf_new, logf_past):
    b, t, nh = logf_new.shape
    lt = jnp.swapaxes(logf_new, 1, 2)
    past = 0
    if logf_past is not None:
        past = logf_past.shape[1]
        lt = jnp.concatenate([jnp.swapaxes(logf_past, 1, 2), lt], axis=2)
    lp = _round_up(past + t, LANES)
    ft = _cumsum_lanes(_pad_to(lt, 2, lp)) * LOG2E
    fk = ft.reshape(b, nh // 2, 2, lp)
    fq = jnp.swapaxes(fk[..., past:past + t], 2, 3)
    return fq, fk


def _rwkv_kernel(r_ref, k_ref, v_ref, wl_ref, a_ref, g_ref, rl_ref, kl_ref, vl_ref, s0_ref,
                 mur_ref, muk_ref, muv_ref, kk_ref, ka_ref, rk_ref, lnw_ref, lnb_ref,
                 y_ref, sout_ref, st_ref, prev_ref):
    itb = pl.program_id(2)
    tb, lanes = r_ref.shape
    n_pairs = lanes // PAIR
    c = min(CHUNK, tb)

    pairs = range(n_pairs)
    heads = range(2)

    @pl.when(itb == 0)
    def _():
        zero = jnp.zeros((HEAD_DIM, HEAD_DIM), F32)
        for p in pairs:
            st_ref[p] = jnp.concatenate([jnp.concatenate([s0_ref[2 * p], zero], axis=1),
                                         jnp.concatenate([zero, s0_ref[2 * p + 1]], axis=1)], axis=0)
        prev_ref[0:1, :] = rl_ref[...]
        prev_ref[1:2, :] = kl_ref[...]
        prev_ref[2:3, :] = vl_ref[...]
    ri = lax.broadcasted_iota(I32, (c, 2 * c), 0)
    ci = lax.broadcasted_iota(I32, (c, 2 * c), 1) % c
    strict2 = ri > ci
    incl2 = ri >= ci
    tri_ones = jnp.where(incl2[:, :c], 1.0, 0.0).astype(BF16)
    li = lax.broadcasted_iota(I32, (PAIR, PAIR), 0) // HEAD_DIM
    lj = lax.broadcasted_iota(I32, (PAIR, PAIR), 1) // HEAD_DIM
    same_head = li == lj
    head_ones = jnp.where(same_head, 1.0, 0.0).astype(BF16)
    first = lax.broadcasted_iota(I32, (c, PAIR), 1) < HEAD_DIM
    row0 = lax.broadcasted_iota(I32, (c, PAIR), 0) == 0
    zeros_cb = jnp.zeros((c, PAIR), BF16)

    def head_sum2(x):
        hi, lo = _split2(x)
        s = _dot(jnp.concatenate([hi, lo], axis=0), head_ones)
        return s[:c] + s[c:]

    def shifted(x, prev_row):
        return jnp.where(row0, prev_row, pltpu.roll(x, 1, 0))

    def by_head(x):
        return [jnp.where(first, x, zeros_cb), jnp.where(first, zeros_cb, x)]

    def chunk(ic, carry):
        rows = pl.ds(pl.multiple_of(ic * c, c), c)
        lns = [slice(pp * PAIR, (pp + 1) * PAIR) for pp in pairs]
        r, k, v, wl, a, kk = [], [], [], [], [], []
        for ln in lns:
            r_raw, k_raw, v_raw = r_ref[rows, ln], k_ref[rows, ln], v_ref[rows, ln]
            r.append(r_raw + mur_ref[:, ln] * (shifted(r_raw, prev_ref[0:1, ln]) - r_raw))
            k.append(k_raw + muk_ref[:, ln] * (shifted(k_raw, prev_ref[1:2, ln]) - k_raw))
            v.append(v_raw + muv_ref[:, ln] * (shifted(v_raw, prev_ref[2:3, ln]) - v_raw))
            prev_ref[0:1, ln] = r_raw[c - 1:c, :]
            prev_ref[1:2, ln] = k_raw[c - 1:c, :]
            prev_ref[2:3, ln] = v_raw[c - 1:c, :]
            wl.append(wl_ref[rows, ln])
            a.append(a_ref[rows, ln])
            kk.append(k[-1] * kk_ref[:, ln])
        lcum = []
        for p in pairs:
            s = _dot(tri_ones, jnp.concatenate(_split3(wl[p]), axis=1))
            lcum.append(s[:, :PAIR] + s[:, PAIR:2 * PAIR] + s[:, 2 * PAIR:])
        kk_ss = [head_sum2(x * x) for x in kk]
        kf = [k[p] * (1.0 + (a[p] - 1.0) * ka_ref[:, lns[p]]) for p in pairs]
        bonus_s = [_dot((r[p] * kf[p] * rk_ref[:, lns[p]]).astype(BF16), head_ones) for p in pairs]
        kk = [kk[p] / jnp.maximum(jnp.sqrt(kk_ss[p]), 1e-12) for p in pairs]
        b = [kk[p] * a[p] for p in pairs]
        inv = [jnp.exp(-lcum[p]) for p in pairs]
        ltot = [lcum[p][c - 1:c, :] for p in pairs]
        suffix = [jnp.exp(ltot[p] - lcum[p]) for p in pairs]
        r_bar = [(r[p] * jnp.exp(lcum[p])).astype(BF16) for p in pairs]
        a_bar = [(-kk[p] * jnp.exp(lcum[p] - wl[p])).astype(BF16) for p in pairs]
        bk_bar = [jnp.concatenate([(b[p] * inv[p]).astype(BF16), (kf[p] * inv[p]).astype(BF16)], axis=0)
                  for p in pairs]
        bk_suf = [jnp.concatenate([(b[p] * suffix[p]).astype(BF16), (kf[p] * suffix[p]).astype(BF16)], axis=0)
                  for p in pairs]
        vb = [x.astype(BF16) for x in v]
        st = [st_ref[p] for p in pairs]
        stb = [x.astype(BF16) for x in st]

        prod = [_dot_nt(jnp.concatenate(by_head(a_bar[p]) + by_head(r_bar[p]), axis=0), bk_bar[p]) for p in pairs]
        a_full = [[jnp.where(strict2, prod[p][e * c:(e + 1) * c], 0.0).astype(BF16) for e in heads] for p in pairs]
        r_full = [[jnp.where(incl2, prod[p][(2 + e) * c:(3 + e) * c], 0.0).astype(BF16) for e in heads]
                  for p in pairs]
        on_s = [_dot_nt(jnp.concatenate([a_bar[p], r_bar[p]], axis=0), stb[p]) for p in pairs]
        zv = [jnp.concatenate([zeros_cb, vb[p]], axis=0) for p in pairs]
        rhs_v = [[_dot(a_full[p][e], zv[p]) for e in heads] for p in pairs]
        x = [[on_s[p][:c] + jnp.where(first, rhs_v[p][0], rhs_v[p][1])] * 2 for p in pairs]
        x = [list(xp) for xp in x]
        ap = [[a_full[p][e][:, :c] for e in heads] for p in pairs]
        span = 1
        while True:
            span *= 2
            if span >= c:
                x = [[x[p][e] + _dot(ap[p][e], x[p][e].astype(BF16)) for e in heads] for p in pairs]
                break
            res = [[_dot(ap[p][e], jnp.concatenate([x[p][e].astype(BF16), ap[p][e]], axis=1)) for e in heads]
                   for p in pairs]
            x = [[x[p][e] + res[p][e][:, :PAIR] for e in heads] for p in pairs]
            ap = [[res[p][e][:, PAIR:].astype(BF16) for e in heads] for p in pairs]
        uv = [jnp.concatenate([jnp.where(first, x[p][0], x[p][1]).astype(BF16), vb[p]], axis=0) for p in pairs]

        y_h = [[_dot(r_full[p][e], uv[p]) for e in heads] for p in pairs]
        upd = [_dot_tn(uv[p], bk_suf[p]) for p in pairs]
        for p in pairs:
            st_ref[p] = st[p] * jnp.exp(ltot[p]) + jnp.where(same_head, upd[p], 0.0)

        y = [on_s[p][c:] + jnp.where(first, y_h[p][0], y_h[p][1]) for p in pairs]
        mean = [head_sum2(y[p]) * (1.0 / HEAD_DIM) for p in pairs]
        dy = [y[p] - mean[p] for p in pairs]
        var = [head_sum2(dy[p] * dy[p]) * (1.0 / HEAD_DIM) for p in pairs]
        for p in pairs:
            ln = lns[p]
            yn = dy[p] * lax.rsqrt(var[p] + GN_EPS) * lnw_ref[:, ln] + lnb_ref[:, ln]
            y_ref[rows, ln] = ((yn + bonus_s[p] * v[p]) * g_ref[rows, ln].astype(F32)).astype(y_ref.dtype)
        return carry

    lax.fori_loop(0, tb // c, chunk, 0)

    @pl.when(itb == pl.num_programs(2) - 1)
    def _():
        for p in pairs:
            st = st_ref[p]
            sout_ref[2 * p] = st[:HEAD_DIM, :HEAD_DIM]
            sout_ref[2 * p + 1] = st[HEAD_DIM:, HEAD_DIM:]


def _rwkv7(r, k, v, wl, a, g, r_last, k_last, v_last, s0, p, pairs_per_step=8):
    b, t, ca = r.shape
    n_pairs = ca // PAIR
    pps = min(pairs_per_step, n_pairs)
    lanes = pps * PAIR
    tb = min(t, 256)
    assert t % tb == 0 and tb % min(CHUNK, tb) == 0
    seq_spec = pl.BlockSpec((None, tb, lanes), lambda ib, ig, it: (ib, it, ig))
    row_spec = pl.BlockSpec((None, 1, lanes), lambda ib, ig, it: (ib, 0, ig))
    st_spec = pl.BlockSpec((None, 2 * pps, HEAD_DIM, HEAD_DIM), lambda ib, ig, it: (ib, ig, 0, 0))
    par_spec = pl.BlockSpec((1, lanes), lambda ib, ig, it: (0, ig))
    mu = p["mu_rkv"].reshape(3, 1, ca)
    vecs = [mu[0], mu[1], mu[2], p["k_k"].reshape(1, ca), p["k_a"].reshape(1, ca), p["r_k"].reshape(1, ca),
            p["ln_x_w"].reshape(1, ca), p["ln_x_b"].reshape(1, ca)]
    return pl.pallas_call(
        _rwkv_kernel,
        out_shape=[jax.ShapeDtypeStruct((b, t, ca), BF16), jax.ShapeDtypeStruct(s0.shape, F32)],
        grid=(b, n_pairs // pps, t // tb),
        in_specs=[seq_spec] * 6 + [row_spec] * 3 + [st_spec] + [par_spec] * 8,
        out_specs=[seq_spec, st_spec],
        scratch_shapes=[pltpu.VMEM((pps, PAIR, PAIR), F32), pltpu.VMEM((8, lanes), F32)],
        compiler_params=_cparams("arbitrary", "arbitrary", "arbitrary"),
        name="rwkv7",
    )(r, k, v, wl, a, g, r_last, k_last, v_last, s0, *vecs)


def _matmul_kernel(a_ref, w_ref, o_ref):
    o_ref[...] = _dot(a_ref[...].astype(BF16), w_ref[...])


def _matmul_rows(a, w_bf16, tn=1024):
    m, kd = a.shape
    n = w_bf16.shape[1]
    assert n % tn == 0
    return pl.pallas_call(
        _matmul_kernel,
        out_shape=jax.ShapeDtypeStruct((m, n), F32),
        grid=(n // tn,),
        in_specs=[pl.BlockSpec((m, kd), lambda j: (0, 0)), pl.BlockSpec((kd, tn), lambda j: (0, j))],
        out_specs=pl.BlockSpec((m, tn), lambda j: (0, j)),
        compiler_params=_cparams("arbitrary"),
        name="matmul_rows",
    )(a, w_bf16)


ROUTER_ROWS = 40


def _rms(x, g):
    return x * lax.rsqrt(jnp.mean(x * x, axis=-1, keepdims=True) + RMS_EPS) * g


U32 = jnp.uint32
HIGH_HALF = 0xFFFF0000


def _pack_halves(x):
    n = x.shape[1] // 2
    lo = lax.bitcast_convert_type(x[:, :n].astype(BF16).astype(F32), U32)
    hi = lax.bitcast_convert_type(x[:, n:].astype(BF16).astype(F32), U32)
    return (hi & U32(HIGH_HALF)) | (lo >> U32(16))


def _unpack_halves(w):
    lo = lax.bitcast_convert_type(w << U32(16), F32)
    hi = lax.bitcast_convert_type(w & U32(HIGH_HALF), F32)
    return lo, hi


def _postmix_kernel(ya_ref, yb_ref, x_ref, gt_ref, sc_ref, sh_ref, gpost_ref, gpre_ref,
                    woa_ref, wob_ref, wrh_ref, wrl_ref, br_ref,
                    x1_ref, h2_ref, eid_ref, wt_ref):
    bb, tt, d = x_ref.shape
    rows = bb * tt
    o = (_dot(ya_ref[...].reshape(rows, ya_ref.shape[-1]), woa_ref[...])
         + _dot(yb_ref[...].reshape(rows, yb_ref.shape[-1]), wob_ref[...])).reshape(bb, tt, d)
    x1 = x_ref[...] + gt_ref[...] * _rms(o, gpost_ref[...])
    x1_ref[...] = x1
    h2 = _mod_norm(x1, gpre_ref[...], sc_ref[...], sh_ref[...]).reshape(rows, d)
    h2_ref[...] = _pack_halves(h2)

    hi = h2.astype(BF16)
    lo = (h2 - hi.astype(F32)).astype(BF16)
    logits = (_dot_nt(wrh_ref[...], hi) + _dot_nt(wrh_ref[...], lo) + _dot_nt(wrl_ref[...], hi)) + br_ref[...]
    le = logits[0:N_EXPERTS, :]
    lg = logits[N_EXPERTS:N_EXPERTS + N_GROUPS, :]
    gio = lax.broadcasted_iota(I32, lg.shape, 0)
    gmax = jnp.max(lg, axis=0, keepdims=True)
    gi = jnp.min(jnp.where(lg == gmax, gio, N_GROUPS), axis=0, keepdims=True)
    pg = 1.0 / jnp.sum(jnp.exp(lg - gmax), axis=0, keepdims=True)
    eio = lax.broadcasted_iota(I32, le.shape, 0)
    le1 = jnp.where(eio // EXPERTS_PER_GROUP == gi, le, NEG_BIG)
    m1 = jnp.max(le1, axis=0, keepdims=True)
    i1 = jnp.min(jnp.where(le1 == m1, eio, N_EXPERTS), axis=0, keepdims=True)
    le2 = jnp.where(eio == i1, NEG_BIG, le1)
    m2 = jnp.max(le2, axis=0, keepdims=True)
    i2 = jnp.min(jnp.where(le2 == m2, eio, N_EXPERTS), axis=0, keepdims=True)
    e2 = jnp.exp(m2 - m1)
    den = 1.0 + e2
    eid_ref[0:1, :] = i1
    eid_ref[1:2, :] = i2
    wt_ref[0:1, :] = pg / den
    wt_ref[1:2, :] = pg * e2 / den


def _postmix(ya, yb, x, gt, sc, sh, g_post, g_pre, wo_a, wo_b, wr_hi, wr_lo, br):
    b, t, d = x.shape
    bb, tt = _row_tiling(b, t)
    rows = bb * tt
    nt = t // tt
    n = b * t

    def row_spec(w):
        return pl.BlockSpec((bb, tt, w), lambda ib, it: (ib, it, 0))

    mod_spec = pl.BlockSpec((bb, 1, d), lambda ib, it: (ib, 0, 0))
    vec_spec = pl.BlockSpec((1, 1, d), lambda ib, it: (0, 0, 0))

    def full(a):
        return pl.BlockSpec(a.shape, lambda ib, it: (0,) * a.ndim)

    tok_spec = pl.BlockSpec((TOP_K, rows), lambda ib, it: (0, ib * nt + it))
    weights = [wo_a, wo_b, wr_hi, wr_lo, br]
    return pl.pallas_call(
        _postmix_kernel,
        out_shape=[jax.ShapeDtypeStruct((b, t, d), F32), jax.ShapeDtypeStruct((n, d // 2), U32),
                   jax.ShapeDtypeStruct((TOP_K, n), I32), jax.ShapeDtypeStruct((TOP_K, n), F32)],
        grid=(b // bb, nt),
        in_specs=[row_spec(ya.shape[-1]), row_spec(yb.shape[-1]), row_spec(d), mod_spec, mod_spec, mod_spec,
                  vec_spec, vec_spec] + [full(a) for a in weights],
        out_specs=[row_spec(d), pl.BlockSpec((rows, d // 2), lambda ib, it: (ib * nt + it, 0)), tok_spec, tok_spec],
        compiler_params=_cparams("arbitrary", "arbitrary"),
        name="postmix",
    )(ya, yb, x, gt, sc, sh, g_post.reshape(1, 1, d), g_pre.reshape(1, 1, d), *weights)


def _rank_kernel(eid_ref, rank_ref, cnt_ref, carry_ref):
    i = pl.program_id(0)
    tr = eid_ref.shape[1]

    @pl.when(i == 0)
    def _():
        carry_ref[...] = jnp.zeros_like(carry_ref)

    r = lax.broadcasted_iota(I32, (tr, tr), 0)
    c = lax.broadcasted_iota(I32, (tr, tr), 1)
    upper = jnp.where(r <= c, 1.0, 0.0).astype(BF16)
    eio = lax.broadcasted_iota(I32, (N_EXPERTS, tr), 0)
    base = carry_ref[...]
    for k in range(TOP_K):
        onehot = jnp.where(eio == eid_ref[k:k + 1, :], 1.0, 0.0)
        cum = _dot(onehot.astype(BF16), upper)
        rank = jnp.sum(onehot * (base + cum - onehot), axis=0, keepdims=True)
        rank_ref[k:k + 1, :] = rank.astype(I32)
        base = base + cum[:, tr - 1:tr]
    carry_ref[...] = base
    cnt_ref[...] = jnp.broadcast_to(base, cnt_ref.shape)


def _moe_rank(eid):
    n = eid.shape[1]
    tr = min(n, 512)
    assert n % tr == 0
    return pl.pallas_call(
        _rank_kernel,
        out_shape=[jax.ShapeDtypeStruct((TOP_K, n), I32), jax.ShapeDtypeStruct((N_EXPERTS, LANES), F32)],
        grid=(n // tr,),
        in_specs=[pl.BlockSpec((TOP_K, tr), lambda i: (0, i))],
        out_specs=[pl.BlockSpec((TOP_K, tr), lambda i: (0, i)),
                   pl.BlockSpec((N_EXPERTS, LANES), lambda i: (0, 0))],
        scratch_shapes=[pltpu.VMEM((N_EXPERTS, 1), F32)],
        compiler_params=_cparams("arbitrary"),
        name="moe_rank",
    )(eid)


def _dispatch_kernel(dest_ref, h_ref, xin0_ref, xin_ref, sem):
    del xin0_ref
    tt = h_ref.shape[0]

    def row_copy(t, dest):
        return pltpu.make_async_copy(h_ref.at[pl.ds(t, 1)], xin_ref.at[pl.ds(dest, 1)], sem)

    def issue(t, carry):
        for k in range(TOP_K):
            row_copy(t, dest_ref[k, t]).start(priority=k % 2)
        return carry

    def drain(t, carry):
        for k in range(TOP_K):
            row_copy(0, 0).wait()
        return carry

    lax.fori_loop(0, tt, issue, 0, unroll=8)
    lax.fori_loop(0, tt, drain, 0, unroll=8)


def _moe_dispatch(dest, h2, n_rows):
    n, d = h2.shape
    tt = min(n, MOE_BLOCK)
    return pl.pallas_call(
        _dispatch_kernel,
        out_shape=jax.ShapeDtypeStruct((n_rows, d), h2.dtype),
        grid=(n // tt,),
        in_specs=[pl.BlockSpec((TOP_K, tt), lambda i: (0, i), memory_space=pltpu.SMEM),
                  pl.BlockSpec((tt, d), lambda i: (i, 0)),
                  pl.BlockSpec(memory_space=pl.ANY)],
        out_specs=pl.BlockSpec(memory_space=pl.ANY),
        scratch_shapes=[pltpu.SemaphoreType.DMA(())],
        input_output_aliases={2: 0},
        compiler_params=_cparams("arbitrary"),
        name="moe_dispatch",
    )(dest, h2, jnp.zeros((n_rows, d), h2.dtype))


def _expert_kernel(be_ref, nu_ref, x_ref, w1_ref, w3_ref, w2_ref, y_ref, w1b_ref, w3b_ref, w2b_ref):
    i = pl.program_id(0)

    @pl.when(jnp.logical_or(i == 0, be_ref[i] != be_ref[jnp.maximum(i - 1, 0)]))
    def _():
        w1b_ref[...] = w1_ref[...].astype(BF16)
        w3b_ref[...] = w3_ref[...].astype(BF16)
        w2b_ref[...] = w2_ref[...].astype(BF16)

    @pl.when(i < nu_ref[0])
    def _():
        x_lo, x_hi = _unpack_halves(x_ref[...])
        n = x_lo.shape[1]
        x_lo, x_hi = x_lo.astype(BF16), x_hi.astype(BF16)
        h1 = _dot(x_lo, w1b_ref[0:n, :]) + _dot(x_hi, w1b_ref[n:2 * n, :])
        h3 = _dot(x_lo, w3b_ref[0:n, :]) + _dot(x_hi, w3b_ref[n:2 * n, :])
        hb = (h1 * _sigmoid(h1)) * h3
        y_ref[...] = _pack_halves(_dot(hb.astype(BF16), w2b_ref[...]))

    @pl.when(i >= nu_ref[0])
    def _():
        y_ref[...] = jnp.zeros_like(y_ref)


def _moe_experts(blk_eid, n_used, xin, w1, w3, w2):
    n_rows, dh = xin.shape
    e, f, d = w2.shape
    up_spec = pl.BlockSpec((None, d, f), lambda i, be, nu: (be[i], 0, 0))
    grid_spec = pltpu.PrefetchScalarGridSpec(
        num_scalar_prefetch=2,
        grid=(n_rows // MOE_BLOCK,),
        in_specs=[pl.BlockSpec((MOE_BLOCK, dh), lambda i, be, nu: (i, 0)), up_spec, up_spec,
                  pl.BlockSpec((None, f, d), lambda i, be, nu: (be[i], 0, 0))],
        out_specs=pl.BlockSpec((MOE_BLOCK, dh), lambda i, be, nu: (i, 0)),
        scratch_shapes=[pltpu.VMEM((d, f), BF16), pltpu.VMEM((d, f), BF16), pltpu.VMEM((f, d), BF16)],
    )
    return pl.pallas_call(
        _expert_kernel,
        out_shape=jax.ShapeDtypeStruct((n_rows, dh), U32),
        grid_spec=grid_spec,
        compiler_params=_cparams("arbitrary"),
        name="moe_experts",
    )(blk_eid, n_used, xin, w1, w3, w2)


def _combine_kernel(dest_ref, wt_ref, y_ref, x1_ref, gt_ref, g_ref, o_ref, buf_ref, sem):
    bb, tt, d = x1_ref.shape
    rows = bb * tt

    def row_copy(k, t, src):
        return pltpu.make_async_copy(y_ref.at[pl.ds(src, 1)], buf_ref.at[k, pl.ds(t, 1)], sem)

    def issue(t, carry):
        for k in range(TOP_K):
            row_copy(k, t, dest_ref[k, t]).start(priority=k % 2)
        return carry

    def drain(t, carry):
        for k in range(TOP_K):
            row_copy(k, 0, 0).wait()
        return carry

    lax.fori_loop(0, rows, issue, 0, unroll=8)
    lax.fori_loop(0, rows, drain, 0, unroll=8)
    wt = wt_ref[...]
    lo0, hi0 = _unpack_halves(buf_ref[0])
    lo1, hi1 = _unpack_halves(buf_ref[1])
    w0, w1 = wt[:, 0:1], wt[:, 1:2]
    f = jnp.concatenate([lo0 * w0 + lo1 * w1, hi0 * w0 + hi1 * w1], axis=1).reshape(bb, tt, d)
    o_ref[...] = x1_ref[...] + gt_ref[...] * _rms(f, g_ref[...])


def _moe_combine(dest, wt_rows, y, x1, gt, g_post):
    b, t, d = x1.shape
    if t >= MOE_BLOCK:
        bb, tt = 1, MOE_BLOCK
    else:
        bb, tt = MOE_BLOCK // t, t
    assert t % tt == 0 and b % bb == 0
    rows = bb * tt
    nt = t // tt
    return pl.pallas_call(
        _combine_kernel,
        out_shape=jax.ShapeDtypeStruct((b, t, d), F32),
        grid=(b // bb, nt),
        in_specs=[pl.BlockSpec((TOP_K, rows), lambda ib, it: (0, ib * nt + it), memory_space=pltpu.SMEM),
                  pl.BlockSpec((rows, TOP_K), lambda ib, it: (ib * nt + it, 0)),
                  pl.BlockSpec(memory_space=pl.ANY),
                  pl.BlockSpec((bb, tt, d), lambda ib, it: (ib, it, 0)),
                  pl.BlockSpec((bb, 1, d), lambda ib, it: (ib, 0, 0)),
                  pl.BlockSpec((1, 1, d), lambda ib, it: (0, 0, 0))],
        out_specs=pl.BlockSpec((bb, tt, d), lambda ib, it: (ib, it, 0)),
        scratch_shapes=[pltpu.VMEM((TOP_K, rows, d // 2), U32), pltpu.SemaphoreType.DMA(())],
        compiler_params=_cparams("arbitrary", "arbitrary"),
        name="moe_combine",
    )(dest, wt_rows, y, x1, gt, g_post.reshape(1, 1, d))


def _hier_moe(h2, eid, wt, x1, gt_f, g_post, w1, w3, w2):
    n = h2.shape[0]
    rank, cnt = _moe_rank(eid)
    counts = cnt[:, 0].astype(I32)
    padded = (counts + MOE_BLOCK - 1) // MOE_BLOCK * MOE_BLOCK
    pad_end = jnp.cumsum(padded)
    pad_start = (pad_end - padded).astype(I32)
    n_blocks = (n * TOP_K + N_EXPERTS * (MOE_BLOCK - 1) + MOE_BLOCK - 1) // MOE_BLOCK
    blk_start = jnp.arange(n_blocks, dtype=I32) * MOE_BLOCK
    blk_eid = jnp.minimum(jnp.sum((pad_end[None, :] <= blk_start[:, None]).astype(I32), axis=1), N_EXPERTS - 1)
    n_used = (pad_end[-1:] // MOE_BLOCK).astype(I32)
    experts = jnp.arange(N_EXPERTS, dtype=I32)[:, None, None]
    dest = jnp.sum(jnp.where(eid[None] == experts, pad_start[:, None, None], 0), axis=0) + rank
    xin = _moe_dispatch(dest, h2, n_blocks * MOE_BLOCK)
    y = _moe_experts(blk_eid, n_used, xin, w1, w3, w2)
    return _moe_combine(dest, jnp.swapaxes(wt, 0, 1), y, x1, gt_f, g_post)


def _layer(x, mods, shift_prev, rkv_last, s0, cache, p, w):
    b, t, d = x.shape
    ca = p["w0"].shape[-1]
    sh_a, sc_a, gt_a, sh_f, sc_f, gt_f = mods
    r, k, v, q, kb, vb = _in_proj(x, sc_a, sh_a, p["g_pre_mix"], w["w_main"], [F32, F32, F32, BF16, F32, F32],
                                  [1.0, 1.0, 1.0, Q_PRESCALE, 1.0, 1.0])
    wl, a, gate, logf, shift_new = _lora_heads(x, sc_a, sh_a, p["g_pre_mix"], shift_prev, p)
    ya, s_new = _rwkv7(r, k, v, wl, a, gate, rkv_last[..., :ca], rkv_last[..., ca:2 * ca], rkv_last[..., 2 * ca:],
                       s0, p)
    if cache is None:
        fq, _ = _forget_layouts(logf, None)
        yb = _fox_prompt(q, kb, vb, fq)
    else:
        k_past, v_past, logf_past = cache
        past = k_past.shape[1]
        fq, fk = _forget_layouts(logf, logf_past)
        yb = _fox_sample(q, kb, vb, jnp.transpose(k_past, (0, 2, 3, 1)), jnp.transpose(v_past, (0, 2, 3, 1)), fq, fk)
    x1, h2, eid, wt = _postmix(ya, yb, x, gt_a, sc_f, sh_f, p["g_post_mix"], p["g_pre_ffn"],
                               w["wo_a"], w["wo_b"], w["wr_hi"], w["wr_lo"], w["br"])
    out = _hier_moe(h2, eid, wt, x1, gt_f, p["g_post_ffn"], p["w1"], p["w3"], p["w2"])
    nh = logf.shape[-1]
    return (out, shift_new.reshape(b, d), s_new,
            kb.reshape(b, t, nh, HEAD_DIM), vb.reshape(b, t, nh, HEAD_DIM), logf)


def kernel(x_prompt, x_sample, state_shift, state_wkv, cache_k, cache_v, cache_logf, c_prompt, c_sample, w_ada, b_ada, g_pre_mix, g_post_mix, g_pre_ffn, g_post_ffn, w_in, b_f, mu_rkv, mu_wag, w0, w_lora_a, w_lora_b, a0, a_lora_a, a_lora_b, g_lora_a, g_lora_b, k_k, k_a, r_k, ln_x_w, ln_x_b, w_out, w_rg, b_rg, w_re, b_re, w1, w3, w2):
    stacked = dict(w_ada=w_ada, b_ada=b_ada, g_pre_mix=g_pre_mix, g_post_mix=g_post_mix, g_pre_ffn=g_pre_ffn,
                   g_post_ffn=g_post_ffn, w_in=w_in, b_f=b_f, mu_rkv=mu_rkv, mu_wag=mu_wag, w0=w0,
                   w_lora_a=w_lora_a, w_lora_b=w_lora_b, a0=a0, a_lora_a=a_lora_a, a_lora_b=a_lora_b,
                   g_lora_a=g_lora_a, g_lora_b=g_lora_b, k_k=k_k, k_a=k_a, r_k=r_k, ln_x_w=ln_x_w,
                   ln_x_b=ln_x_b, w_out=w_out, w_rg=w_rg, b_rg=b_rg, w_re=w_re, b_re=b_re, w1=w1, w3=w3, w2=w2)
    depth = w_ada.shape[0]
    assert depth == 1, "one layer per call"
    p = {name: arr[0] for name, arr in stacked.items()}
    bp, _, d = x_prompt.shape
    bs = x_sample.shape[0]
    ca = p["w0"].shape[-1]
    n_main = 3 * ca + 3 * (d - ca)
    p["w_forget"] = p["w_in"][:, n_main:]

    wr = _pad_to(jnp.concatenate([p["w_re"], p["w_rg"]], axis=1).T, 0, ROUTER_ROWS)
    wr_hi = wr.astype(BF16)
    w = dict(
        w_main=p["w_in"][:, :n_main].astype(BF16),
        wo_a=p["w_out"][:ca].astype(BF16), wo_b=p["w_out"][ca:].astype(BF16),
        wr_hi=wr_hi, wr_lo=(wr - wr_hi.astype(F32)).astype(BF16),
        br=_pad_to(jnp.concatenate([p["b_re"], p["b_rg"]]), 0, ROUTER_ROWS).reshape(ROUTER_ROWS, 1),
    )

    mod = _ada_mod(jnp.concatenate([c_prompt, c_sample], axis=0), p["w_ada"], p["b_ada"])
    mods = [mod[:, None, i * d:(i + 1) * d] for i in range(6)]
    mods_p = [m[:bp] for m in mods]
    mods_s = [m[bp:] for m in mods]

    out_p = _layer(x_prompt, mods_p, jnp.zeros((bp, 1, d), F32), jnp.zeros((bp, 1, 3 * ca), F32),
                   jnp.zeros((bp,) + state_wkv.shape[2:], F32), None, p, w)
    rkv_last = _matmul_rows(state_shift[0], w["w_main"][:, :3 * ca], tn=ca)[:, None, :]
    out_s = _layer(x_sample, mods_s, state_shift[0][:, None, :], rkv_last, state_wkv[0],
                   (cache_k[0], cache_v[0], cache_logf[0]), p, w)
    yp, shp, wkvp, kp, vp, lfp = out_p
    ys, shs, wkvs, ks_, vs_, lfs = out_s
    return (yp, ys, shp[None], wkvp[None], kp[None], vp[None], lfp[None],
            shs[None], wkvs[None], ks_[None], vs_[None], lfs[None])
```
